```python
import math
import jax, jax.numpy as jnp
from jax import lax
import numpy as np

D_MODEL = 2048
BATCH = 4
SEQ = 2048
DEPTH = 4

CHUNK = 64
N_MIXERS = 3
N_POOL_LAYERS = (DEPTH + 2) // 3
N_SB_LAYERS = (DEPTH + 1) // 3
N_SSM_LAYERS = DEPTH // 3

POOL_WINDOWS = (2, 4, 8, 16)
N_POOL_GROUPS = len(POOL_WINDOWS)
POOL_GROUP_DIM = D_MODEL // N_POOL_GROUPS

SB_HEAD_DIM = 128
SB_HEADS = D_MODEL // SB_HEAD_DIM
Q_BLOCK = 128

SSM_GROUP_CH = 16
SSM_GROUPS = D_MODEL // SSM_GROUP_CH
SSM_STATE = 64
SSM_DT_MIN = 1e-3
SSM_DT_MAX = 1e-1

D_FF = 5632
CONV_WIDTH = 3

RMS_EPS = 1e-6

kernel_name = "hybrid_pool_stickbreak_s5_convffn_trunk"


def rms_norm(x, g):
    xf = x.astype(jnp.float32)
    y = xf * lax.rsqrt(jnp.mean(xf * xf, axis=-1, keepdims=True) + RMS_EPS)
    return (y * g.astype(jnp.float32)).astype(x.dtype)


def multiscale_pool_mixer(h, w, b, scale):
    bsz, seq, _ = h.shape
    hf = h.astype(jnp.float32).reshape(bsz, seq, N_POOL_GROUPS, POOL_GROUP_DIM)
    cs = jnp.cumsum(hf, axis=1)
    cs = jnp.concatenate([jnp.zeros_like(cs[:, :1]), cs], axis=1)
    t = jnp.arange(seq)[:, None]
    win = jnp.array(POOL_WINDOWS, dtype=jnp.int32)[None, :]
    lo = jnp.maximum(t + 1 - win, 0)
    cnt = (t + 1 - lo).astype(jnp.float32)
    grp = jnp.arange(N_POOL_GROUPS)[None, :]
    lower = cs[:, lo, grp]
    mean = (cs[:, 1:] - lower) / cnt[None, :, :, None]
    pooled = mean - hf
    y = jnp.einsum('bsgc,gcd->bsgd', pooled, w.astype(jnp.float32))
    y = y.reshape(bsz, seq, D_MODEL) + b.astype(jnp.float32)
    return (y * scale.astype(jnp.float32)).astype(h.dtype)


def stick_breaking_attention(h, w_qkv, q_gain, k_gain, w_o):
    bsz, seq, _ = h.shape
    qkv = (h @ w_qkv).reshape(bsz, seq, 3, SB_HEADS, SB_HEAD_DIM)
    q = rms_norm(qkv[:, :, 0], q_gain).astype(jnp.float32).transpose(0, 2, 1, 3)
    k = rms_norm(qkv[:, :, 1], k_gain).astype(jnp.float32).transpose(0, 2, 1, 3)
    v = qkv[:, :, 2].transpose(0, 2, 1, 3)
    inv_sqrt_d = 1.0 / math.sqrt(SB_HEAD_DIM)
    outs = []
    for blk in range(seq // Q_BLOCK):
        q0 = blk * Q_BLOCK
        kv_len = q0 + Q_BLOCK
        qb = q[:, :, q0:kv_len]
        kb = k[:, :, :kv_len]
        vb = v[:, :, :kv_len]
        z = jnp.einsum('bhqd,bhkd->bhqk', qb, kb) * inv_sqrt_d
        t_idx = q0 + jnp.arange(Q_BLOCK)[:, None]
        s_idx = jnp.arange(kv_len)[None, :]
        mask = s_idx < t_idx
        log_beta = jax.nn.log_sigmoid(z)
        log_1m_beta = jnp.where(mask, jax.nn.log_sigmoid(-z), 0.0)
        log_remain = lax.cumsum(log_1m_beta, axis=3, reverse=True) - log_1m_beta
        attn = jnp.where(mask, jnp.exp(log_beta + log_remain), 0.0)
        outs.append(jnp.einsum('bhqk,bhkd->bhqd', attn.astype(vb.dtype), vb))
    o = jnp.concatenate(outs, axis=2)
    o = o.transpose(0, 2, 1, 3).reshape(bsz, seq, D_MODEL)
    return o @ w_o


def _ssm_combine(e1, e2):
    a1r, a1i, b1r, b1i = e1
    a2r, a2i, b2r, b2i = e2
    return (a2r * a1r - a2i * a1i,
            a2r * a1i + a2i * a1r,
            a2r * b1r - a2i * b1i + b2r,
            a2r * b1i + a2i * b1r + b2i)


def s5_mixer(h, lam_re, lam_im, log_step, b_re, b_im, c_re, c_im, d_skip, w_glu, b_glu):
    bsz, seq, _ = h.shape
    u = h.astype(jnp.float32).reshape(bsz, seq, SSM_GROUPS, SSM_GROUP_CH)
    lr = lam_re.astype(jnp.float32)
    li = lam_im.astype(jnp.float32)
    step = jnp.exp(log_step.astype(jnp.float32))[:, None]
    mag = jnp.exp(lr * step)
    lb_re = mag * jnp.cos(li * step)
    lb_im = mag * jnp.sin(li * step)
    den = lr * lr + li * li
    f_re = ((lb_re - 1.0) * lr + lb_im * li) / den
    f_im = (lb_im * lr - (lb_re - 1.0) * li) / den
    br = b_re.astype(jnp.float32)
    bi = b_im.astype(jnp.float32)
    bb_re = f_re[..., None] * br - f_im[..., None] * bi
    bb_im = f_re[..., None] * bi + f_im[..., None] * br
    bu_re = jnp.einsum('bsgh,gph->bsgp', u, bb_re)
    bu_im = jnp.einsum('bsgh,gph->bsgp', u, bb_im)
    a_re = jnp.broadcast_to(lb_re, bu_re.shape)
    a_im = jnp.broadcast_to(lb_im, bu_im.shape)
    _, _, xs_re, xs_im = lax.associative_scan(_ssm_combine, (a_re, a_im, bu_re, bu_im), axis=1)
    y = (jnp.einsum('bsgp,ghp->bsgh', xs_re, c_re.astype(jnp.float32))
         - jnp.einsum('bsgp,ghp->bsgh', xs_im, c_im.astype(jnp.float32))
         + d_skip.astype(jnp.float32).reshape(SSM_GROUPS, SSM_GROUP_CH) * u)
    y = jax.nn.gelu(y.reshape(bsz, seq, D_MODEL)).astype(h.dtype)
    gv = y @ w_glu + b_glu
    val, gate = jnp.split(gv, 2, axis=-1)
    return val * jax.nn.sigmoid(gate)


def conv_ffn(h, w_up, conv_w, conv_b, w_down):
    seq = h.shape[1]
    up = h @ w_up
    padded = jnp.pad(up, ((0, 0), (CONV_WIDTH - 1, 0), (0, 0)))
    c = conv_b + sum(conv_w[j] * padded[:, j:j + seq] for j in range(CONV_WIDTH))
    val, gate = jnp.split(c, 2, axis=-1)
    return (jax.nn.silu(gate) * val) @ w_down


def setup_inputs(seed: int = 0) -> dict:
    key = jax.random.key(seed)
    ks = jax.random.split(key, 26)
    f32 = jnp.float32
    nrm = lambda k, shape, s: jax.random.normal(k, shape, f32) * s
    lam_im_base = jnp.pi * jnp.arange(SSM_STATE, dtype=f32)
    return {
        "x": jax.random.normal(ks[0], (BATCH, SEQ, D_MODEL), f32),
        "norm_mix_g": 1.0 + nrm(ks[1], (DEPTH, D_MODEL), 0.02),
        "norm_ffn_g": 1.0 + nrm(ks[2], (DEPTH, D_MODEL), 0.02),
        "pool_w": nrm(ks[3], (N_POOL_LAYERS, N_POOL_GROUPS, POOL_GROUP_DIM, POOL_GROUP_DIM), POOL_GROUP_DIM ** -0.5),
        "pool_b": nrm(ks[4], (N_POOL_LAYERS, D_MODEL), 0.01),
        "pool_scale": 1.0 + nrm(ks[5], (N_POOL_LAYERS, D_MODEL), 0.02),
        "sb_w_qkv": nrm(ks[6], (N_SB_LAYERS, D_MODEL, 3 * D_MODEL), D_MODEL ** -0.5),
        "sb_q_gain": 1.0 + nrm(ks[7], (N_SB_LAYERS, SB_HEAD_DIM), 0.02),
        "sb_k_gain": 1.0 + nrm(ks[8], (N_SB_LAYERS, SB_HEAD_DIM), 0.02),
        "sb_w_o": nrm(ks[9], (N_SB_LAYERS, D_MODEL, D_MODEL), D_MODEL ** -0.5),
        "ssm_lam_re": -0.5 + nrm(ks[10], (N_SSM_LAYERS, SSM_GROUPS, SSM_STATE), 0.01),
        "ssm_lam_im": lam_im_base + nrm(ks[11], (N_SSM_LAYERS, SSM_GROUPS, SSM_STATE), 0.01),
        "ssm_log_step": jax.random.uniform(ks[12], (N_SSM_LAYERS, SSM_GROUPS), f32,
                                           math.log(SSM_DT_MIN), math.log(SSM_DT_MAX)),
        "ssm_b_re": nrm(ks[13], (N_SSM_LAYERS, SSM_GROUPS, SSM_STATE, SSM_GROUP_CH), (2 * SSM_GROUP_CH) ** -0.5),
        "ssm_b_im": nrm(ks[14], (N_SSM_LAYERS, SSM_GROUPS, SSM_STATE, SSM_GROUP_CH), (2 * SSM_GROUP_CH) ** -0.5),
        "ssm_c_re": nrm(ks[15], (N_SSM_LAYERS, SSM_GROUPS, SSM_GROUP_CH, SSM_STATE), (2 * SSM_STATE) ** -0.5),
        "ssm_c_im": nrm(ks[16], (N_SSM_LAYERS, SSM_GROUPS, SSM_GROUP_CH, SSM_STATE), (2 * SSM_STATE) ** -0.5),
        "ssm_d": nrm(ks[17], (N_SSM_LAYERS, D_MODEL), 1.0),
        "ssm_w_glu": nrm(ks[18], (N_SSM_LAYERS, D_MODEL, 2 * D_MODEL), D_MODEL ** -0.5),
        "ssm_b_glu": nrm(ks[19], (N_SSM_LAYERS, 2 * D_MODEL), 0.01),
        "ffn_w_up": nrm(ks[20], (DEPTH, D_MODEL, 2 * D_FF), D_MODEL ** -0.5),
        "ffn_conv_w": nrm(ks[21], (DEPTH, CONV_WIDTH, 2 * D_FF), CONV_WIDTH ** -0.5),
        "ffn_conv_b": nrm(ks[22], (DEPTH, 2 * D_FF), 0.01),
        "ffn_w_down": nrm(ks[23], (DEPTH, D_FF, D_MODEL), D_FF ** -0.5),
    }


def reference(x, norm_mix_g, norm_ffn_g, pool_w, pool_b, pool_scale,
              sb_w_qkv, sb_q_gain, sb_k_gain, sb_w_o,
              ssm_lam_re, ssm_lam_im, ssm_log_step, ssm_b_re, ssm_b_im,
              ssm_c_re, ssm_c_im, ssm_d, ssm_w_glu, ssm_b_glu,
              ffn_w_up, ffn_conv_w, ffn_conv_b, ffn_w_down):
    for i in range(DEPTH):
        kind = i % N_MIXERS
        j = i // N_MIXERS
        h = rms_norm(x, norm_mix_g[i])
        if kind == 0:
            m = multiscale_pool_mixer(h, pool_w[j], pool_b[j], pool_scale[j])
        elif kind == 1:
            m = stick_breaking_attention(h, sb_w_qkv[j], sb_q_gain[j], sb_k_gain[j], sb_w_o[j])
        else:
            m = s5_mixer(h, ssm_lam_re[j], ssm_lam_im[j], ssm_log_step[j], ssm_b_re[j], ssm_b_im[j],
                         ssm_c_re[j], ssm_c_im[j], ssm_d[j], ssm_w_glu[j], ssm_b_glu[j])
        x = x + m
        x = x + conv_ffn(rms_norm(x, norm_ffn_g[i]), ffn_w_up[i], ffn_conv_w[i], ffn_conv_b[i], ffn_w_down[i])
    return x
```

```python
import functools
import math

import jax
import jax.numpy as jnp
from jax import lax
from jax.experimental import pallas as pl
from jax.experimental.pallas import tpu as pltpu

RMS_EPS = 1e-6
POOL_WINDOWS = (2, 4, 8, 16)
POOL_HALO = 16
SB_HEAD_DIM = 128
SSM_GROUP_CH = 16
SSM_STATE = 64
SSM_GROUPS_PER_BLOCK = 16
SCAN_ROWS = 8
CONV_WIDTH = 3

V7X_VMEM_LIMIT_BYTES = 56 * 1024 * 1024

F32 = jnp.float32
BF16 = jnp.bfloat16


def _params(*semantics):
    return pltpu.CompilerParams(dimension_semantics=semantics,
                                vmem_limit_bytes=V7X_VMEM_LIMIT_BYTES)


def _rms(x, g):
    ms = jnp.mean(x * x, axis=-1, keepdims=True)
    return x * lax.rsqrt(ms + RMS_EPS) * g


def _dot(a, b):
    return jnp.dot(a, b, preferred_element_type=F32)


def _ffn_kernel(x_ref, g_ref, wv_ref, wg_ref, cwv_ref, cwg_ref, cbv_ref, cbg_ref, wd_ref,
                o_ref, h_scr, tail_v, tail_g, *, tiles_per_seq):
    i = pl.program_id(0)
    j = pl.program_id(1)
    tm = x_ref.shape[0]
    fc = wv_ref.shape[1]

    @pl.when(j == 0)
    def _():
        x = x_ref[...]
        h_scr[...] = _rms(x, g_ref[...]).astype(BF16)
        o_ref[...] = x

    seq_start = (i % tiles_per_seq) == 0
    h = h_scr[...]
    row = lax.broadcasted_iota(jnp.int32, (SCAN_ROWS, fc), 0)

    def causal_conv(up, tail_ref, cw_ref, cb_ref):
        @pl.when(seq_start)
        def _():
            tail_ref[j] = jnp.zeros((SCAN_ROWS, fc), F32)
        prev = tail_ref[j]
        tail_ref[j] = up[tm - SCAN_ROWS:, :]
        out = cb_ref[...] + cw_ref[2:3, :] * up
        for lag in (1, 2):
            shifted = pltpu.roll(up, lag, axis=0)
            head = jnp.where(row < lag, pltpu.roll(prev, lag, axis=0), shifted[:SCAN_ROWS])
            shifted = jnp.concatenate([head, shifted[SCAN_ROWS:]], axis=0)
            out = out + cw_ref[2 - lag:3 - lag, :] * shifted
        return out

    val = causal_conv(_dot(h, wv_ref[...]), tail_v, cwv_ref, cbv_ref)
    gate = causal_conv(_dot(h, wg_ref[...]), tail_g, cwg_ref, cbg_ref)
    act = (gate * jax.nn.sigmoid(gate) * val).astype(BF16)
    o_ref[...] += _dot(act, wd_ref[...])


def _conv_ffn(x2d, g, w_up, conv_w, conv_b, w_down, *, seq, tm=512, fc=512):
    t, d = x2d.shape
    f = w_down.shape[0]
    nf = f // fc
    assert t % tm == 0 and seq % tm == 0 and f % fc == 0
    kernel = functools.partial(_ffn_kernel, tiles_per_seq=seq // tm)
    return pl.pallas_call(
        kernel,
        grid=(t // tm, nf),
        in_specs=[
            pl.BlockSpec((tm, d), lambda i, j: (i, 0)),
            pl.BlockSpec((1, d), lambda i, j: (0, 0)),
            pl.BlockSpec((d, fc), lambda i, j: (0, j)),
            pl.BlockSpec((d, fc), lambda i, j: (0, nf + j)),
            pl.BlockSpec((CONV_WIDTH, fc), lambda i, j: (0, j)),
            pl.BlockSpec((CONV_WIDTH, fc), lambda i, j: (0, nf + j)),
            pl.BlockSpec((1, fc), lambda i, j: (0, j)),
            pl.BlockSpec((1, fc), lambda i, j: (0, nf + j)),
            pl.BlockSpec((fc, d), lambda i, j: (j, 0)),
        ],
        out_specs=pl.BlockSpec((tm, d), lambda i, j: (i, 0)),
        out_shape=jax.ShapeDtypeStruct((t, d), F32),
        scratch_shapes=[
            pltpu.VMEM((tm, d), BF16),
            pltpu.VMEM((nf, SCAN_ROWS, fc), F32),
            pltpu.VMEM((nf, SCAN_ROWS, fc), F32),
        ],
        compiler_params=_params("arbitrary", "arbitrary"),
        name="conv_ffn",
    )(x2d, g.reshape(1, d), w_up, w_up, conv_w, conv_w, conv_b.reshape(1, -1),
      conv_b.reshape(1, -1), w_down)


def _pool_kernel(x_ref, halo_ref, g_ref, w_ref, b_ref, s_ref, o_ref, hbuf, *, tiles_per_seq):
    i = pl.program_id(0)
    tm, d = x_ref.shape
    dg = d // len(POOL_WINDOWS)
    x = x_ref[...]
    g = g_ref[...]
    h = _rms(x, g)
    tile_in_seq = i % tiles_per_seq
    halo = jnp.where(tile_in_seq == 0, 0.0, _rms(halo_ref[...], g))
    hbuf[0:POOL_HALO, :] = halo
    hbuf[POOL_HALO:, :] = h
    t_in_seq = tile_in_seq * tm + lax.broadcasted_iota(jnp.int32, (tm, dg), 0)
    for gi, win in enumerate(POOL_WINDOWS):
        cols = slice(gi * dg, (gi + 1) * dg)
        hg = h[:, cols]
        acc = hg
        for lag in range(1, win):
            acc = acc + hbuf[POOL_HALO - lag:POOL_HALO - lag + tm, cols]
        cnt = jnp.minimum(t_in_seq + 1, win).astype(F32)
        pooled = acc / cnt - hg
        y = _dot(pooled.astype(BF16), w_ref[gi])
        o_ref[:, cols] = x[:, cols] + (y + b_ref[:, cols]) * s_ref[:, cols]


def _pool_mixer(x2d, g, w, b, scale, *, seq, tm=512):
    t, d = x2d.shape
    ng, dg, _ = w.shape
    assert t % tm == 0 and seq % tm == 0 and tm % POOL_HALO == 0
    halo_blocks_per_tile = tm // POOL_HALO
    kernel = functools.partial(_pool_kernel, tiles_per_seq=seq // tm)
    return pl.pallas_call(
        kernel,
        grid=(t // tm,),
        in_specs=[
            pl.BlockSpec((tm, d), lambda i: (i, 0)),
            pl.BlockSpec((POOL_HALO, d), lambda i: (jnp.maximum(i * halo_blocks_per_tile - 1, 0), 0)),
            pl.BlockSpec((1, d), lambda i: (0, 0)),
            pl.BlockSpec((ng, dg, dg), lambda i: (0, 0, 0)),
            pl.BlockSpec((1, d), lambda i: (0, 0)),
            pl.BlockSpec((1, d), lambda i: (0, 0)),
        ],
        out_specs=pl.BlockSpec((tm, d), lambda i: (i, 0)),
        out_shape=jax.ShapeDtypeStruct((t, d), F32),
        scratch_shapes=[pltpu.VMEM((POOL_HALO + tm, d), F32)],
        compiler_params=_params("arbitrary"),
        name="pool_mixer",
    )(x2d, x2d, g.reshape(1, d), w, b.reshape(1, d), scale.reshape(1, d))


def _qkv_kernel(x_ref, g_ref, w_ref, gain_ref, o_ref, h_scr, *, norm_tiles):
    j = pl.program_id(1)
    tn = w_ref.shape[1]

    @pl.when(j == 0)
    def _():
        h_scr[...] = _rms(x_ref[...], g_ref[...]).astype(BF16)

    acc = _dot(h_scr[...], w_ref[...])

    @pl.when(j < norm_tiles)
    def _():
        for hh in range(tn // SB_HEAD_DIM):
            cols = slice(hh * SB_HEAD_DIM, (hh + 1) * SB_HEAD_DIM)
            o_ref[:, cols] = _rms(acc[:, cols], gain_ref[:, cols]).astype(BF16)

    @pl.when(j >= norm_tiles)
    def _():
        o_ref[...] = acc.astype(BF16)


def _qkv_proj(x2d, g, w_qkv, gain_row, *, tm=1024, tn=512):
    t, d = x2d.shape
    n = w_qkv.shape[1]
    assert t % tm == 0 and n % tn == 0 and (2 * d) % tn == 0 and tn % SB_HEAD_DIM == 0
    kernel = functools.partial(_qkv_kernel, norm_tiles=2 * d // tn)
    return pl.pallas_call(
        kernel,
        grid=(t // tm, n // tn),
        in_specs=[
            pl.BlockSpec((tm, d), lambda i, j: (i, 0)),
            pl.BlockSpec((1, d), lambda i, j: (0, 0)),
            pl.BlockSpec((d, tn), lambda i, j: (0, j)),
            pl.BlockSpec((1, tn), lambda i, j: (0, j)),
        ],
        out_specs=pl.BlockSpec((tm, tn), lambda i, j: (i, j)),
        out_shape=jax.ShapeDtypeStruct((t, n), BF16),
        scratch_shapes=[pltpu.VMEM((tm, d), BF16)],
        compiler_params=_params("arbitrary", "arbitrary"),
        name="sb_qkv",
    )(x2d, g.reshape(1, d), w_qkv, gain_row)


def _sb_attn_kernel(q_ref, k_ref, v_ref, o_ref):
    qi = pl.program_id(2)
    blk, dh = q_ref.shape
    q = q_ref[...]
    row = lax.broadcasted_iota(jnp.int32, (blk, blk), 0)
    col = lax.broadcasted_iota(jnp.int32, (blk, blk), 1)
    later = jnp.where(row > col, 1.0, 0.0).astype(BF16)
    causal = col < row

    def key_block(kb, carry, acc, diagonal):
        start = pl.multiple_of(kb * blk, blk)
        k = k_ref[pl.ds(start, blk), :]
        v = v_ref[pl.ds(start, blk), :]
        z = lax.dot_general(q, k, (((1,), (1,)), ((), ())), preferred_element_type=F32)
        softplus_neg_abs = jnp.log1p(jnp.exp(-jnp.abs(z)))
        log_beta = jnp.minimum(z, 0.0) - softplus_neg_abs
        log_1m_beta = -jnp.maximum(z, 0.0) - softplus_neg_abs
        if diagonal:
            log_1m_beta = jnp.where(causal, log_1m_beta, 0.0)
        hi = log_1m_beta.astype(BF16)
        lo = (log_1m_beta - hi.astype(F32)).astype(BF16)
        remain = _dot(hi, later) + _dot(lo, later)
        w = jnp.exp(log_beta + remain + carry)
        if diagonal:
            w = jnp.where(causal, w, 0.0)
        acc = acc + _dot(w.astype(BF16), v)
        carry = carry + jnp.sum(log_1m_beta, axis=-1, keepdims=True)
        return carry, acc

    carry, acc = key_block(qi, jnp.zeros((blk, 1), F32), jnp.zeros((blk, dh), F32), True)

    def body(n, state):
        return key_block(qi - 1 - n, state[0], state[1], False)

    carry, acc = lax.fori_loop(0, qi, body, (carry, acc))
    o_ref[...] = acc.astype(BF16)


def _sb_attention(qkv3d, *, blk=256):
    bsz, seq, n3 = qkv3d.shape
    d = n3 // 3
    heads = d // SB_HEAD_DIM
    assert seq % blk == 0
    return pl.pallas_call(
        _sb_attn_kernel,
        grid=(bsz, heads, seq // blk),
        in_specs=[
            pl.BlockSpec((None, blk, SB_HEAD_DIM), lambda b, h, qi: (b, qi, h)),
            pl.BlockSpec((None, seq, SB_HEAD_DIM), lambda b, h, qi: (b, 0, heads + h)),
            pl.BlockSpec((None, seq, SB_HEAD_DIM), lambda b, h, qi: (b, 0, 2 * heads + h)),
        ],
        out_specs=pl.BlockSpec((None, blk, SB_HEAD_DIM), lambda b, h, qi: (b, qi, h)),
        out_shape=jax.ShapeDtypeStruct((bsz, seq, d), BF16),
        compiler_params=_params("arbitrary", "arbitrary", "arbitrary"),
        name="sb_attention",
    )(qkv3d, qkv3d, qkv3d)


def _proj_res_kernel(a_ref, w_ref, r_ref, o_ref):
    o_ref[...] = r_ref[...] + _dot(a_ref[...], w_ref[...])


def _proj_residual(a2d, w, res2d, *, tm=1024, tn=512):
    t, k = a2d.shape
    n = w.shape[1]
    assert t % tm == 0 and n % tn == 0
    return pl.pallas_call(
        _proj_res_kernel,
        grid=(t // tm, n // tn),
        in_specs=[
            pl.BlockSpec((tm, k), lambda i, j: (i, 0)),
            pl.BlockSpec((k, tn), lambda i, j: (0, j)),
            pl.BlockSpec((tm, tn), lambda i, j: (i, j)),
        ],
        out_specs=pl.BlockSpec((tm, tn), lambda i, j: (i, j)),
        out_shape=jax.ShapeDtypeStruct((t, n), F32),
        compiler_params=_params("arbitrary", "arbitrary"),
        name="proj_residual",
    )(a2d, w, res2d)


def _rmsnorm_kernel(x_ref, g_ref, o_ref):
    o_ref[...] = _rms(x_ref[...], g_ref[...])


def _rmsnorm(x2d, g, *, tm=512):
    t, d = x2d.shape
    assert t % tm == 0
    return pl.pallas_call(
        _rmsnorm_kernel,
        grid=(t // tm,),
        in_specs=[pl.BlockSpec((tm, d), lambda i: (i, 0)), pl.BlockSpec((1, d), lambda i: (0, 0))],
        out_specs=pl.BlockSpec((tm, d), lambda i: (i, 0)),
        out_shape=jax.ShapeDtypeStruct((t, d), F32),
        compiler_params=_params("arbitrary"),
        name="rmsnorm",
    )(x2d, g.reshape(1, d))


def _s5_kernel(u_ref, wb_ref, tab_ref, wc_ref, dskip_ref, o_ref, st_scr, carry_scr):
    tc = u_ref.shape[0]
    ns = wb_ref.shape[1] // 2
    u = u_ref[...]
    st_scr[...] = _dot(u.astype(BF16), wb_ref[...])

    @pl.when(pl.program_id(2) == 0)
    def _():
        carry_scr[...] = jnp.zeros(carry_scr.shape, F32)

    def slab(s, carry):
        cr, ci = carry
        r0 = pl.multiple_of(s * SCAN_ROWS, SCAN_ROWS)
        xr = st_scr[pl.ds(r0, SCAN_ROWS), 0:ns]
        xi = st_scr[pl.ds(r0, SCAN_ROWS), ns:2 * ns]
        for step, lag in enumerate((1, 2, 4)):
            ar = tab_ref[2 * step]
            ai = tab_ref[2 * step + 1]
            sr = pltpu.roll(xr, lag, axis=0)
            si = pltpu.roll(xi, lag, axis=0)
            xr, xi = xr + ar * sr - ai * si, xi + ar * si + ai * sr
        pr = tab_ref[6]
        pi = tab_ref[7]
        xr, xi = xr + pr * cr - pi * ci, xi + pr * ci + pi * cr
        st_scr[pl.ds(r0, SCAN_ROWS), 0:ns] = xr
        st_scr[pl.ds(r0, SCAN_ROWS), ns:2 * ns] = xi
        last = SCAN_ROWS - 1
        return (jnp.broadcast_to(xr[last:, :], (SCAN_ROWS, ns)),
                jnp.broadcast_to(xi[last:, :], (SCAN_ROWS, ns)))

    cr, ci = lax.fori_loop(0, tc // SCAN_ROWS, slab, (carry_scr[0], carry_scr[1]))
    carry_scr[0] = cr
    carry_scr[1] = ci
    y = _dot(st_scr[...].astype(BF16), wc_ref[...]) + dskip_ref[...] * u
    o_ref[...] = jax.nn.gelu(y, approximate=True).astype(BF16)


def _s5_core(h2d, wb, tab, wc, d_skip, *, bsz, seq, tc=512):
    t, d = h2d.shape
    nblk, cb, ns2 = wb.shape
    assert seq % tc == 0 and tc % SCAN_ROWS == 0 and nblk * cb == d
    nt = seq // tc
    return pl.pallas_call(
        _s5_kernel,
        grid=(bsz, nblk, nt),
        in_specs=[
            pl.BlockSpec((tc, cb), lambda b, gb, tt: (b * nt + tt, gb)),
            pl.BlockSpec((None, cb, ns2), lambda b, gb, tt: (gb, 0, 0)),
            pl.BlockSpec((None, 8, SCAN_ROWS, ns2 // 2), lambda b, gb, tt: (gb, 0, 0, 0)),
            pl.BlockSpec((None, ns2, cb), lambda b, gb, tt: (gb, 0, 0)),
            pl.BlockSpec((1, cb), lambda b, gb, tt: (0, gb)),
        ],
        out_specs=pl.BlockSpec((tc, cb), lambda b, gb, tt: (b * nt + tt, gb)),
        out_shape=jax.ShapeDtypeStruct((t, d), BF16),
        scratch_shapes=[
            pltpu.VMEM((tc, ns2), F32),
            pltpu.VMEM((2, SCAN_ROWS, ns2 // 2), F32),
        ],
        compiler_params=_params("arbitrary", "arbitrary", "arbitrary"),
        name="s5_scan",
    )(h2d, wb, tab, wc, d_skip.reshape(1, d))


def _glu_res_kernel(a_ref, wv_ref, wg_ref, bv_ref, bg_ref, r_ref, o_ref):
    a = a_ref[...]
    val = _dot(a, wv_ref[...]) + bv_ref[...]
    gate = _dot(a, wg_ref[...]) + bg_ref[...]
    o_ref[...] = r_ref[...] + val * jax.nn.sigmoid(gate)


def _glu_residual(a2d, w, b, res2d, *, tm=1024, tn=512):
    t, k = a2d.shape
    n = w.shape[1] // 2
    assert t % tm == 0 and n % tn == 0
    nj = n // tn
    return pl.pallas_call(
        _glu_res_kernel,
        grid=(t // tm, nj),
        in_specs=[
            pl.BlockSpec((tm, k), lambda i, j: (i, 0)),
            pl.BlockSpec((k, tn), lambda i, j: (0, j)),
            pl.BlockSpec((k, tn), lambda i, j: (0, nj + j)),
            pl.BlockSpec((1, tn), lambda i, j: (0, j)),
            pl.BlockSpec((1, tn), lambda i, j: (0, nj + j)),
            pl.BlockSpec((tm, tn), lambda i, j: (i, j)),
        ],
        out_specs=pl.BlockSpec((tm, tn), lambda i, j: (i, j)),
        out_shape=jax.ShapeDtypeStruct((t, n), F32),
        compiler_params=_params("arbitrary", "arbitrary"),
        name="glu_residual",
    )(a2d, w, w, b.reshape(1, -1), b.reshape(1, -1), res2d)


def _s5_tables(lam_re, lam_im, log_step, b_re, b_im, c_re, c_im):
    groups, p = lam_re.shape
    hc = b_re.shape[2]
    gpb = SSM_GROUPS_PER_BLOCK
    nblk = groups // gpb
    step = jnp.exp(log_step)[:, None]
    mag = jnp.exp(lam_re * step)
    a_re = mag * jnp.cos(lam_im * step)
    a_im = mag * jnp.sin(lam_im * step)
    den = lam_re * lam_re + lam_im * lam_im
    f_re = ((a_re - 1.0) * lam_re + a_im * lam_im) / den
    f_im = (a_im * lam_re - (a_re - 1.0) * lam_im) / den
    bb_re = f_re[..., None] * b_re - f_im[..., None] * b_im
    bb_im = f_re[..., None] * b_im + f_im[..., None] * b_re
    eye = jnp.eye(gpb, dtype=F32)

    def block_diag_in(bb):
        bb = bb.reshape(nblk, gpb, p, hc).transpose(0, 1, 3, 2)
        return jnp.einsum('bghp,gk->bghkp', bb, eye).reshape(nblk, gpb * hc, gpb * p)

    def block_diag_out(c):
        c = c.reshape(nblk, gpb, hc, p).transpose(0, 1, 3, 2)
        return jnp.einsum('bgpo,gk->bgpko', c, eye).reshape(nblk, gpb * p, gpb * hc)

    wb = jnp.concatenate([block_diag_in(bb_re), block_diag_in(bb_im)], axis=2).astype(BF16)
    wc = jnp.concatenate([block_diag_out(c_re), -block_diag_out(c_im)], axis=1).astype(BF16)

    ar = a_re.reshape(nblk, gpb * p)
    ai = a_im.reshape(nblk, gpb * p)
    pow_re, pow_im = [ar], [ai]
    for _ in range(SCAN_ROWS - 1):
        pr, pi = pow_re[-1], pow_im[-1]
        pow_re.append(pr * ar - pi * ai)
        pow_im.append(pr * ai + pi * ar)
    rows = jnp.arange(SCAN_ROWS)[None, :, None]
    tabs = []
    for lag in (1, 2, 4):
        for pw in (pow_re[lag - 1], pow_im[lag - 1]):
            tabs.append(jnp.where(rows >= lag, pw[:, None, :], 0.0))
    tabs.append(jnp.stack(pow_re, axis=1))
    tabs.append(jnp.stack(pow_im, axis=1))
    tab = jnp.stack(tabs, axis=1)
    return wb, tab, wc


def kernel(x, norm_mix_g, norm_ffn_g, pool_w, pool_b, pool_scale, sb_w_qkv, sb_q_gain, sb_k_gain, sb_w_o, ssm_lam_re, ssm_lam_im, ssm_log_step, ssm_b_re, ssm_b_im, ssm_c_re, ssm_c_im, ssm_d, ssm_w_glu, ssm_b_glu, ffn_w_up, ffn_conv_w, ffn_conv_b, ffn_w_down):
    bsz, seq, d = x.shape
    depth = norm_mix_g.shape[0]
    heads = d // SB_HEAD_DIM
    x2d = x.reshape(bsz * seq, d)
    for i in range(depth):
        kind = i % 3
        j = i // 3
        if kind == 0:
            x2d = _pool_mixer(x2d, norm_mix_g[i], pool_w[j].astype(BF16), pool_b[j], pool_scale[j],
                              seq=seq)
        elif kind == 1:
            gain_row = jnp.concatenate([
                jnp.tile(sb_q_gain[j] * (1.0 / math.sqrt(SB_HEAD_DIM)), heads),
                jnp.tile(sb_k_gain[j], heads),
                jnp.ones((d,), F32)]).reshape(1, 3 * d)
            qkv = _qkv_proj(x2d, norm_mix_g[i], sb_w_qkv[j].astype(BF16), gain_row)
            o = _sb_attention(qkv.reshape(bsz, seq, 3 * d))
            x2d = _proj_residual(o.reshape(bsz * seq, d), sb_w_o[j].astype(BF16), x2d)
        else:
            wb, tab, wc = _s5_tables(ssm_lam_re[j], ssm_lam_im[j], ssm_log_step[j], ssm_b_re[j],
                                     ssm_b_im[j], ssm_c_re[j], ssm_c_im[j])
            h = _rmsnorm(x2d, norm_mix_g[i])
            y = _s5_core(h, wb, tab, wc, ssm_d[j], bsz=bsz, seq=seq)
            x2d = _glu_residual(y, ssm_w_glu[j].astype(BF16), ssm_b_glu[j], x2d)
        x2d = _conv_ffn(x2d, norm_ffn_g[i], ffn_w_up[i].astype(BF16), ffn_conv_w[i], ffn_conv_b[i],
                        ffn_w_down[i].astype(BF16), seq=seq)
    return x2d.reshape(bsz, seq, d)
```

```python
import functools
import math

import jax
import jax.numpy as jnp
from jax import lax
from jax.experimental import pallas as pl
from jax.experimental.pallas import tpu as pltpu

RMS_EPS = 1e-6
POOL_WINDOWS = (2, 4, 8, 16)
POOL_HALO = 16
SB_HEAD_DIM = 128
SSM_GROUP_CH = 16
SSM_STATE = 64
SSM_GROUPS_PER_BLOCK = 16
SCAN_ROWS = 8
CONV_WIDTH = 3

V7X_VMEM_LIMIT_BYTES = 56 * 1024 * 1024

F32 = jnp.float32
BF16 = jnp.bfloat16


def _params(*semantics):
    return pltpu.CompilerParams(dimension_semantics=semantics,
                                vmem_limit_bytes=V7X_VMEM_LIMIT_BYTES)


def _rms(x, g):
    ms = jnp.mean(x * x, axis=-1, keepdims=True)
    return x * lax.rsqrt(ms + RMS_EPS) * g


def _dot(a, b):
    return jnp.dot(a, b, preferred_element_type=F32)


def _ffn_kernel(x_ref, g_ref, wv_ref, wg_ref, cwv_ref, cwg_ref, cbv_ref, cbg_ref, wd_ref,
                o_ref, h_scr, act_a, act_b, tail_v, tail_g, *, tiles_per_seq):
    i = pl.program_id(0)
    j = pl.program_id(1)
    n_tiles = pl.num_programs(0) - 1
    nf = pl.num_programs(1)
    tm = x_ref.shape[0]
    fc = wv_ref.shape[1]
    step = i * nf + j

    @pl.when(step == 0)
    def _():
        act_b[...] = jnp.zeros(act_b.shape, BF16)
        o_ref[...] = jnp.zeros(o_ref.shape, F32)

    @pl.when((j == 0) & (i < n_tiles))
    def _():
        h_scr[...] = _rms(x_ref[...], g_ref[...]).astype(BF16)

        @pl.when(i % tiles_per_seq == 0)
        def _():
            tail_v[...] = jnp.zeros(tail_v.shape, F32)
            tail_g[...] = jnp.zeros(tail_g.shape, F32)

    @pl.when((j == 1) & (i < n_tiles))
    def _():
        o_ref[...] = x_ref[...]

    chunk = jnp.where(i < n_tiles, j, nf - 1)
    active = (i < n_tiles) | (j == 0)
    row = lax.broadcasted_iota(jnp.int32, (SCAN_ROWS, fc), 0)

    def causal_conv(up, tail_ref, cw_ref, cb_ref):
        prev = tail_ref[chunk]
        tail_ref[chunk] = up[tm - SCAN_ROWS:, :]
        out = cb_ref[...] + cw_ref[2:3, :] * up
        for lag in (1, 2):
            shifted = pltpu.roll(up, lag, axis=0)
            head = jnp.where(row < lag, pltpu.roll(prev, lag, axis=0), shifted[:SCAN_ROWS])
            shifted = jnp.concatenate([head, shifted[SCAN_ROWS:]], axis=0)
            out = out + cw_ref[2 - lag:3 - lag, :] * shifted
        return out

    def stages(act_new, act_old):
        h = h_scr[...]
        val = causal_conv(_dot(h, wv_ref[...]), tail_v, cwv_ref, cbv_ref)
        up_gate = _dot(h, wg_ref[...])
        o_ref[...] += _dot(act_old[...], wd_ref[...])
        gate = causal_conv(up_gate, tail_g, cwg_ref, cbg_ref)
        act_new[...] = (gate * jax.nn.sigmoid(gate) * val).astype(BF16)

    pl.when(active & (step % 2 == 0))(lambda: stages(act_a, act_b))
    pl.when(active & (step % 2 == 1))(lambda: stages(act_b, act_a))


def _conv_ffn(x2d, g, w_up, conv_w, conv_b, w_down, *, layer, seq, tm=512, fc=512):
    t, d = x2d.shape
    f = w_down.shape[1]
    nf = f // fc
    n_tiles = t // tm
    assert t % tm == 0 and seq % tm == 0 and f % fc == 0
    kernel = functools.partial(_ffn_kernel, tiles_per_seq=seq // tm)

    def up_chunk(i, j):
        return jnp.where(i < n_tiles, j, nf - 1)

    def down_chunk(i, j):
        return jnp.where((i < n_tiles) & (j > 0), j - 1, nf - 1)

    def down_tile(i, j):
        return jnp.clip(jnp.where(j == 0, i - 1, i), 0, n_tiles - 1)

    return pl.pallas_call(
        kernel,
        grid=(n_tiles + 1, nf),
        in_specs=[
            pl.BlockSpec((tm, d), lambda i, j: (jnp.minimum(i, n_tiles - 1), 0)),
            pl.BlockSpec((1, d), lambda i, j: (0, 0)),
            pl.BlockSpec((None, d, fc), lambda i, j: (layer, 0, up_chunk(i, j))),
            pl.BlockSpec((None, d, fc), lambda i, j: (layer, 0, nf + up_chunk(i, j))),
            pl.BlockSpec((None, CONV_WIDTH, fc), lambda i, j: (layer, 0, up_chunk(i, j))),
            pl.BlockSpec((None, CONV_WIDTH, fc), lambda i, j: (layer, 0, nf + up_chunk(i, j))),
            pl.BlockSpec((None, 1, fc), lambda i, j: (layer, 0, up_chunk(i, j))),
            pl.BlockSpec((None, 1, fc), lambda i, j: (layer, 0, nf + up_chunk(i, j))),
            pl.BlockSpec((None, fc, d), lambda i, j: (layer, down_chunk(i, j), 0)),
        ],
        out_specs=pl.BlockSpec((tm, d), lambda i, j: (down_tile(i, j), 0)),
        out_shape=jax.ShapeDtypeStruct((t, d), F32),
        scratch_shapes=[
            pltpu.VMEM((tm, d), BF16),
            pltpu.VMEM((tm, fc), BF16),
            pltpu.VMEM((tm, fc), BF16),
            pltpu.VMEM((nf, SCAN_ROWS, fc), F32),
            pltpu.VMEM((nf, SCAN_ROWS, fc), F32),
        ],
        compiler_params=_params("arbitrary", "arbitrary"),
        name="conv_ffn",
    )(x2d, g.reshape(1, d), w_up, w_up, conv_w, conv_w, conv_b, conv_b, w_down)


def _pool_kernel(x_ref, halo_ref, g_ref, w_ref, b_ref, s_ref, o_ref, hbuf, *, tiles_per_seq):
    i = pl.program_id(0)
    tm, d = x_ref.shape
    dg = d // len(POOL_WINDOWS)
    x = x_ref[...]
    g = g_ref[...]
    h = _rms(x, g)
    tile_in_seq = i % tiles_per_seq
    halo = jnp.where(tile_in_seq == 0, 0.0, _rms(halo_ref[...], g))
    hbuf[0:POOL_HALO, :] = halo
    hbuf[POOL_HALO:, :] = h
    t_in_seq = tile_in_seq * tm + lax.broadcasted_iota(jnp.int32, (tm, dg), 0)
    for gi, win in enumerate(POOL_WINDOWS):
        cols = slice(gi * dg, (gi + 1) * dg)
        hg = h[:, cols]
        acc = hg
        for lag in range(1, win):
            acc = acc + hbuf[POOL_HALO - lag:POOL_HALO - lag + tm, cols]
        cnt = jnp.minimum(t_in_seq + 1, win).astype(F32)
        pooled = acc / cnt - hg
        y = _dot(pooled.astype(BF16), w_ref[gi])
        o_ref[:, cols] = x[:, cols] + (y + b_ref[:, cols]) * s_ref[:, cols]


def _pool_mixer(x2d, g, w, b, scale, *, seq, tm=512):
    t, d = x2d.shape
    ng, dg, _ = w.shape
    assert t % tm == 0 and seq % tm == 0 and tm % POOL_HALO == 0
    halo_blocks_per_tile = tm // POOL_HALO
    kernel = functools.partial(_pool_kernel, tiles_per_seq=seq // tm)
    return pl.pallas_call(
        kernel,
        grid=(t // tm,),
        in_specs=[
            pl.BlockSpec((tm, d), lambda i: (i, 0)),
            pl.BlockSpec((POOL_HALO, d), lambda i: (jnp.maximum(i * halo_blocks_per_tile - 1, 0), 0)),
            pl.BlockSpec((1, d), lambda i: (0, 0)),
            pl.BlockSpec((ng, dg, dg), lambda i: (0, 0, 0)),
            pl.BlockSpec((1, d), lambda i: (0, 0)),
            pl.BlockSpec((1, d), lambda i: (0, 0)),
        ],
        out_specs=pl.BlockSpec((tm, d), lambda i: (i, 0)),
        out_shape=jax.ShapeDtypeStruct((t, d), F32),
        scratch_shapes=[pltpu.VMEM((POOL_HALO + tm, d), F32)],
        compiler_params=_params("arbitrary"),
        name="pool_mixer",
    )(x2d, x2d, g.reshape(1, d), w, b.reshape(1, d), scale.reshape(1, d))


def _qkv_kernel(x_ref, g_ref, w_ref, gain_ref, o_ref, h_scr, *, norm_tiles):
    j = pl.program_id(1)
    tn = w_ref.shape[1]

    @pl.when(j == 0)
    def _():
        h_scr[...] = _rms(x_ref[...], g_ref[...]).astype(BF16)

    acc = _dot(h_scr[...], w_ref[...])

    @pl.when(j < norm_tiles)
    def _():
        for hh in range(tn // SB_HEAD_DIM):
            cols = slice(hh * SB_HEAD_DIM, (hh + 1) * SB_HEAD_DIM)
            o_ref[:, cols] = _rms(acc[:, cols], gain_ref[:, cols]).astype(BF16)

    @pl.when(j >= norm_tiles)
    def _():
        o_ref[...] = acc.astype(BF16)


def _qkv_proj(x2d, g, w_qkv, gain_row, *, tm=1024, tn=512):
    t, d = x2d.shape
    n = w_qkv.shape[1]
    assert t % tm == 0 and n % tn == 0 and (2 * d) % tn == 0 and tn % SB_HEAD_DIM == 0
    kernel = functools.partial(_qkv_kernel, norm_tiles=2 * d // tn)
    return pl.pallas_call(
        kernel,
        grid=(t // tm, n // tn),
        in_specs=[
            pl.BlockSpec((tm, d), lambda i, j: (i, 0)),
            pl.BlockSpec((1, d), lambda i, j: (0, 0)),
            pl.BlockSpec((d, tn), lambda i, j: (0, j)),
            pl.BlockSpec((1, tn), lambda i, j: (0, j)),
        ],
        out_specs=pl.BlockSpec((tm, tn), lambda i, j: (i, j)),
        out_shape=jax.ShapeDtypeStruct((t, n), BF16),
        scratch_shapes=[pltpu.VMEM((tm, d), BF16)],
        compiler_params=_params("arbitrary", "arbitrary"),
        name="sb_qkv",
    )(x2d, g.reshape(1, d), w_qkv, gain_row)


def _sb_attn_kernel(q_ref, k_ref, v_ref, o_ref, *, chain_rows):
    qi = pl.program_id(2)
    blk = q_ref.shape[0]
    dh = SB_HEAD_DIM
    n_heads = q_ref.shape[1] // dh
    row = lax.broadcasted_iota(jnp.int32, (blk, blk), 0)
    col = lax.broadcasted_iota(jnp.int32, (blk, blk), 1)
    later = jnp.where(row > col, 1.0, 0.0).astype(BF16)
    later2 = jnp.concatenate([later, later], axis=0)
    row_g = lax.broadcasted_iota(jnp.int32, (chain_rows, blk), 0)
    col_g = lax.broadcasted_iota(jnp.int32, (chain_rows, blk), 1)

    chains = [(hh, r0) for hh in range(n_heads) for r0 in range(0, blk, chain_rows)]

    def key_block(kb, state, diagonal):
        start = pl.multiple_of(kb * blk, blk)
        cols = [slice(hh * dh, (hh + 1) * dh) for hh, _ in chains]
        rows = [slice(r0, r0 + chain_rows) for _, r0 in chains]
        zs = [lax.dot_general(q_ref[r, c], k_ref[pl.ds(start, blk), c], (((1,), (1,)), ((), ())),
                              preferred_element_type=F32) for r, c in zip(rows, cols)]
        log_betas, log_1m_betas, splits = [], [], []
        for (_, r0), z in zip(chains, zs):
            log_beta = jnp.minimum(z, 0.0) - jnp.log(1.0 + jnp.exp(-jnp.abs(z)))
            log_1m_beta = log_beta - z
            if diagonal:
                log_1m_beta = jnp.where(col_g < row_g + r0, log_1m_beta, 0.0)
            hi = log_1m_beta.astype(BF16)
            lo = (log_1m_beta - hi.astype(F32)).astype(BF16)
            log_betas.append(log_beta)
            log_1m_betas.append(log_1m_beta)
            splits.append(jnp.concatenate([hi, lo], axis=1))
        remains = [_dot(s, later2) for s in splits]
        ws = []
        for (_, r0), log_beta, remain, (carry, _) in zip(chains, log_betas, remains, state):
            w = jnp.exp(log_beta + remain + carry)
            if diagonal:
                w = jnp.where(col_g < row_g + r0, w, 0.0)
            ws.append(w.astype(BF16))
        out = []
        for c, w, log_1m_beta, (carry, acc) in zip(cols, ws, log_1m_betas, state):
            out.append((carry + jnp.sum(log_1m_beta, axis=-1, keepdims=True),
                        acc + _dot(w, v_ref[pl.ds(start, blk), c])))
        return tuple(out)

    zero = (jnp.zeros((chain_rows, 1), F32), jnp.zeros((chain_rows, dh), F32))
    state = key_block(qi, (zero,) * len(chains), True)
    state = lax.fori_loop(0, qi, lambda n, st: key_block(qi - 1 - n, st, False), state)
    for (hh, r0), (_, acc) in zip(chains, state):
        o_ref[r0:r0 + chain_rows, hh * dh:(hh + 1) * dh] = acc.astype(BF16)


def _sb_attention(qkv3d, *, blk=256, heads_per_step=4, chain_rows=256):
    bsz, seq, n3 = qkv3d.shape
    d = n3 // 3
    wide = heads_per_step * SB_HEAD_DIM
    groups = d // wide
    assert seq % blk == 0 and d % wide == 0 and blk % chain_rows == 0
    return pl.pallas_call(
        functools.partial(_sb_attn_kernel, chain_rows=chain_rows),
        grid=(bsz, groups, seq // blk),
        in_specs=[
            pl.BlockSpec((None, blk, wide), lambda b, h, qi: (b, qi, h)),
            pl.BlockSpec((None, seq, wide), lambda b, h, qi: (b, 0, groups + h)),
            pl.BlockSpec((None, seq, wide), lambda b, h, qi: (b, 0, 2 * groups + h)),
        ],
        out_specs=pl.BlockSpec((None, blk, wide), lambda b, h, qi: (b, qi, h)),
        out_shape=jax.ShapeDtypeStruct((bsz, seq, d), BF16),
        compiler_params=_params("arbitrary", "arbitrary", "arbitrary"),
        name="sb_attention",
    )(qkv3d, qkv3d, qkv3d)


def _proj_res_kernel(a_ref, w_ref, r_ref, o_ref):
    o_ref[...] = r_ref[...] + _dot(a_ref[...], w_ref[...])


def _proj_residual(a2d, w, res2d, *, tm=1024, tn=512):
    t, k = a2d.shape
    n = w.shape[1]
    assert t % tm == 0 and n % tn == 0
    return pl.pallas_call(
        _proj_res_kernel,
        grid=(t // tm, n // tn),
        in_specs=[
            pl.BlockSpec((tm, k), lambda i, j: (i, 0)),
            pl.BlockSpec((k, tn), lambda i, j: (0, j)),
            pl.BlockSpec((tm, tn), lambda i, j: (i, j)),
        ],
        out_specs=pl.BlockSpec((tm, tn), lambda i, j: (i, j)),
        out_shape=jax.ShapeDtypeStruct((t, n), F32),
        compiler_params=_params("arbitrary", "arbitrary"),
        name="proj_residual",
    )(a2d, w, res2d)


def _rmsnorm_kernel(x_ref, g_ref, o_ref):
    o_ref[...] = _rms(x_ref[...], g_ref[...])


def _rmsnorm(x2d, g, *, tm=512):
    t, d = x2d.shape
    assert t % tm == 0
    return pl.pallas_call(
        _rmsnorm_kernel,
        grid=(t // tm,),
        in_specs=[pl.BlockSpec((tm, d), lambda i: (i, 0)), pl.BlockSpec((1, d), lambda i: (0, 0))],
        out_specs=pl.BlockSpec((tm, d), lambda i: (i, 0)),
        out_shape=jax.ShapeDtypeStruct((t, d), F32),
        compiler_params=_params("arbitrary"),
        name="rmsnorm",
    )(x2d, g.reshape(1, d))


def _s5_kernel(u_ref, wb_ref, tab_ref, wc_ref, dskip_ref, o_ref, st_scr, carry_scr):
    tc = u_ref.shape[0]
    ns = wb_ref.shape[1] // 2
    u = u_ref[...]
    st_scr[...] = _dot(u.astype(BF16), wb_ref[...])

    @pl.when(pl.program_id(2) == 0)
    def _():
        carry_scr[...] = jnp.zeros(carry_scr.shape, F32)

    def slab(s, carry):
        cr, ci = carry
        r0 = pl.multiple_of(s * SCAN_ROWS, SCAN_ROWS)
        xr = st_scr[pl.ds(r0, SCAN_ROWS), 0:ns]
        xi = st_scr[pl.ds(r0, SCAN_ROWS), ns:2 * ns]
        for step, lag in enumerate((1, 2, 4)):
            ar = tab_ref[2 * step]
            ai = tab_ref[2 * step + 1]
            sr = pltpu.roll(xr, lag, axis=0)
            si = pltpu.roll(xi, lag, axis=0)
            xr, xi = xr + ar * sr - ai * si, xi + ar * si + ai * sr
        pr = tab_ref[6]
        pi = tab_ref[7]
        xr, xi = xr + pr * cr - pi * ci, xi + pr * ci + pi * cr
        st_scr[pl.ds(r0, SCAN_ROWS), 0:ns] = xr
        st_scr[pl.ds(r0, SCAN_ROWS), ns:2 * ns] = xi
        last = SCAN_ROWS - 1
        return (jnp.broadcast_to(xr[last:, :], (SCAN_ROWS, ns)),
                jnp.broadcast_to(xi[last:, :], (SCAN_ROWS, ns)))

    cr, ci = lax.fori_loop(0, tc // SCAN_ROWS, slab, (carry_scr[0], carry_scr[1]))
    carry_scr[0] = cr
    carry_scr[1] = ci
    y = _dot(st_scr[...].astype(BF16), wc_ref[...]) + dskip_ref[...] * u
    o_ref[...] = jax.nn.gelu(y, approximate=True).astype(BF16)


def _s5_core(h2d, wb, tab, wc, d_skip, *, bsz, seq, tc=512):
    t, d = h2d.shape
    nblk, cb, ns2 = wb.shape
    assert seq % tc == 0 and tc % SCAN_ROWS == 0 and nblk * cb == d
    nt = seq // tc
    return pl.pallas_call(
        _s5_kernel,
        grid=(bsz, nblk, nt),
        in_specs=[
            pl.BlockSpec((tc, cb), lambda b, gb, tt: (b * nt + tt, gb)),
            pl.BlockSpec((None, cb, ns2), lambda b, gb, tt: (gb, 0, 0)),
            pl.BlockSpec((None, 8, SCAN_ROWS, ns2 // 2), lambda b, gb, tt: (gb, 0, 0, 0)),
            pl.BlockSpec((None, ns2, cb), lambda b, gb, tt: (gb, 0, 0)),
            pl.BlockSpec((1, cb), lambda b, gb, tt: (0, gb)),
        ],
        out_specs=pl.BlockSpec((tc, cb), lambda b, gb, tt: (b * nt + tt, gb)),
        out_shape=jax.ShapeDtypeStruct((t, d), BF16),
        scratch_shapes=[
            pltpu.VMEM((tc, ns2), F32),
            pltpu.VMEM((2, SCAN_ROWS, ns2 // 2), F32),
        ],
        compiler_params=_params("arbitrary", "arbitrary", "arbitrary"),
        name="s5_scan",
    )(h2d, wb, tab, wc, d_skip.reshape(1, d))


def _glu_res_kernel(a_ref, wv_ref, wg_ref, bv_ref, bg_ref, r_ref, o_ref):
    a = a_ref[...]
    val = _dot(a, wv_ref[...]) + bv_ref[...]
    gate = _dot(a, wg_ref[...]) + bg_ref[...]
    o_ref[...] = r_ref[...] + val * jax.nn.sigmoid(gate)


def _glu_residual(a2d, w, b, res2d, *, tm=1024, tn=512):
    t, k = a2d.shape
    n = w.shape[1] // 2
    assert t % tm == 0 and n % tn == 0
    nj = n // tn
    return pl.pallas_call(
        _glu_res_kernel,
        grid=(t // tm, nj),
        in_specs=[
            pl.BlockSpec((tm, k), lambda i, j: (i, 0)),
            pl.BlockSpec((k, tn), lambda i, j: (0, j)),
            pl.BlockSpec((k, tn), lambda i, j: (0, nj + j)),
            pl.BlockSpec((1, tn), lambda i, j: (0, j)),
            pl.BlockSpec((1, tn), lambda i, j: (0, nj + j)),
            pl.BlockSpec((tm, tn), lambda i, j: (i, j)),
        ],
        out_specs=pl.BlockSpec((tm, tn), lambda i, j: (i, j)),
        out_shape=jax.ShapeDtypeStruct((t, n), F32),
        compiler_params=_params("arbitrary", "arbitrary"),
        name="glu_residual",
    )(a2d, w, w, b.reshape(1, -1), b.reshape(1, -1), res2d)


def _s5_tables(lam_re, lam_im, log_step, b_re, b_im, c_re, c_im):
    groups, p = lam_re.shape
    hc = b_re.shape[2]
    gpb = SSM_GROUPS_PER_BLOCK
    nblk = groups // gpb
    step = jnp.exp(log_step)[:, None]
    mag = jnp.exp(lam_re * step)
    a_re = mag * jnp.cos(lam_im * step)
    a_im = mag * jnp.sin(lam_im * step)
    den = lam_re * lam_re + lam_im * lam_im
    f_re = ((a_re - 1.0) * lam_re + a_im * lam_im) / den
    f_im = (a_im * lam_re - (a_re - 1.0) * lam_im) / den
    bb_re = f_re[..., None] * b_re - f_im[..., None] * b_im
    bb_im = f_re[..., None] * b_im + f_im[..., None] * b_re
    eye = jnp.eye(gpb, dtype=F32)

    def block_diag_in(bb):
        bb = bb.reshape(nblk, gpb, p, hc).transpose(0, 1, 3, 2)
        return jnp.einsum('bghp,gk->bghkp', bb, eye).reshape(nblk, gpb * hc, gpb * p)

    def block_diag_out(c):
        c = c.reshape(nblk, gpb, hc, p).transpose(0, 1, 3, 2)
        return jnp.einsum('bgpo,gk->bgpko', c, eye).reshape(nblk, gpb * p, gpb * hc)

    wb = jnp.concatenate([block_diag_in(bb_re), block_diag_in(bb_im)], axis=2).astype(BF16)
    wc = jnp.concatenate([block_diag_out(c_re), -block_diag_out(c_im)], axis=1).astype(BF16)

    ar = a_re.reshape(nblk, gpb * p)
    ai = a_im.reshape(nblk, gpb * p)
    pow_re, pow_im = [ar], [ai]
    for _ in range(SCAN_ROWS - 1):
        pr, pi = pow_re[-1], pow_im[-1]
        pow_re.append(pr * ar - pi * ai)
        pow_im.append(pr * ai + pi * ar)
    rows = jnp.arange(SCAN_ROWS)[None, :, None]
    tabs = []
    for lag in (1, 2, 4):
        for pw in (pow_re[lag - 1], pow_im[lag - 1]):
            tabs.append(jnp.where(rows >= lag, pw[:, None, :], 0.0))
    tabs.append(jnp.stack(pow_re, axis=1))
    tabs.append(jnp.stack(pow_im, axis=1))
    tab = jnp.stack(tabs, axis=1)
    return wb, tab, wc


def kernel(x, norm_mix_g, norm_ffn_g, pool_w, pool_b, pool_scale, sb_w_qkv, sb_q_gain, sb_k_gain, sb_w_o, ssm_lam_re, ssm_lam_im, ssm_log_step, ssm_b_re, ssm_b_im, ssm_c_re, ssm_c_im, ssm_d, ssm_w_glu, ssm_b_glu, ffn_w_up, ffn_conv_w, ffn_conv_b, ffn_w_down):
    bsz, seq, d = x.shape
    depth = norm_mix_g.shape[0]
    heads = d // SB_HEAD_DIM
    x2d = x.reshape(bsz * seq, d)
    ffn_w_up = ffn_w_up.astype(BF16)
    ffn_w_down = ffn_w_down.astype(BF16)
    ffn_conv_b = ffn_conv_b.reshape(depth, 1, -1)
    for i in range(depth):
        kind = i % 3
        j = i // 3
        if kind == 0:
            x2d = _pool_mixer(x2d, norm_mix_g[i], pool_w[j].astype(BF16), pool_b[j], pool_scale[j],
                              seq=seq)
        elif kind == 1:
            gain_row = jnp.concatenate([
                jnp.tile(sb_q_gain[j] * (1.0 / math.sqrt(SB_HEAD_DIM)), heads),
                jnp.tile(sb_k_gain[j], heads),
                jnp.ones((d,), F32)]).reshape(1, 3 * d)
            qkv = _qkv_proj(x2d, norm_mix_g[i], sb_w_qkv[j].astype(BF16), gain_row)
            o = _sb_attention(qkv.reshape(bsz, seq, 3 * d))
            x2d = _proj_residual(o.reshape(bsz * seq, d), sb_w_o[j].astype(BF16), x2d)
        else:
            wb, tab, wc = _s5_tables(ssm_lam_re[j], ssm_lam_im[j], ssm_log_step[j], ssm_b_re[j],
                                     ssm_b_im[j], ssm_c_re[j], ssm_c_im[j])
            h = _rmsnorm(x2d, norm_mix_g[i])
            y = _s5_core(h, wb, tab, wc, ssm_d[j], bsz=bsz, seq=seq)
            x2d = _glu_residual(y, ssm_w_glu[j].astype(BF16), ssm_b_glu[j], x2d)
        x2d = _conv_ffn(x2d, norm_ffn_g[i], ffn_w_up, ffn_conv_w, ffn_conv_b, ffn_w_down,
                        layer=i, seq=seq)
    return x2d.reshape(bsz, seq, d)
```

```python
import functools
import math

import jax
import jax.numpy as jnp
from jax import lax
from jax.experimental import pallas as pl
from jax.experimental.pallas import tpu as pltpu

RMS_EPS = 1e-6
POOL_WINDOWS = (2, 4, 8, 16)
POOL_HALO = 16
SB_HEAD_DIM = 128
SSM_GROUP_CH = 16
SSM_STATE = 64
SSM_GROUPS_PER_BLOCK = 16
SCAN_ROWS = 8
S5_TIME_CHUNK = 512
LANES = 128
CONV_WIDTH = 3

V7X_VMEM_LIMIT_BYTES = 60 * 1024 * 1024

F32 = jnp.float32
BF16 = jnp.bfloat16


def _params(*semantics):
    return pltpu.CompilerParams(dimension_semantics=semantics,
                                vmem_limit_bytes=V7X_VMEM_LIMIT_BYTES)


def _rms(x, g):
    ms = jnp.mean(x * x, axis=-1, keepdims=True)
    return x * lax.rsqrt(ms + RMS_EPS) * g


def _dot(a, b):
    return jnp.dot(a, b, preferred_element_type=F32)


def _ffn_kernel(x_ref, g_ref, wv_ref, wg_ref, cwv_ref, cwg_ref, cbv_ref, cbg_ref, wd_ref,
                o_ref, h_scr, act_a, act_b, tail_v, tail_g, *, tiles_per_seq):
    i = pl.program_id(0)
    j = pl.program_id(1)
    n_tiles = pl.num_programs(0) - 1
    nf = pl.num_programs(1)
    tm = x_ref.shape[0]
    fc = wv_ref.shape[1]
    step = i * nf + j

    @pl.when(step == 0)
    def _():
        act_b[...] = jnp.zeros(act_b.shape, BF16)
        o_ref[...] = jnp.zeros(o_ref.shape, F32)

    @pl.when((j == 0) & (i < n_tiles))
    def _():
        h_scr[...] = _rms(x_ref[...], g_ref[...]).astype(BF16)

        @pl.when(i % tiles_per_seq == 0)
        def _():
            tail_v[...] = jnp.zeros(tail_v.shape, F32)
            tail_g[...] = jnp.zeros(tail_g.shape, F32)

    @pl.when((j == 1) & (i < n_tiles))
    def _():
        o_ref[...] = x_ref[...]

    chunk = jnp.where(i < n_tiles, j, nf - 1)
    active = (i < n_tiles) | (j == 0)
    row = lax.broadcasted_iota(jnp.int32, (SCAN_ROWS, fc), 0)

    def causal_conv(up, tail_ref, cw_ref, cb_ref):
        prev = tail_ref[chunk]
        tail_ref[chunk] = up[tm - SCAN_ROWS:, :]
        out = cb_ref[...] + cw_ref[2:3, :] * up
        for lag in (1, 2):
            shifted = pltpu.roll(up, lag, axis=0)
            head = jnp.where(row < lag, pltpu.roll(prev, lag, axis=0), shifted[:SCAN_ROWS])
            shifted = jnp.concatenate([head, shifted[SCAN_ROWS:]], axis=0)
            out = out + cw_ref[2 - lag:3 - lag, :] * shifted
        return out

    def stages(act_new, act_old):
        h = h_scr[...]
        val = causal_conv(_dot(h, wv_ref[...]), tail_v, cwv_ref, cbv_ref)
        up_gate = _dot(h, wg_ref[...])
        o_ref[...] += _dot(act_old[...], wd_ref[...])
        gate = causal_conv(up_gate, tail_g, cwg_ref, cbg_ref)
        act_new[...] = (gate * jax.nn.sigmoid(gate) * val).astype(BF16)

    pl.when(active & (step % 2 == 0))(lambda: stages(act_a, act_b))
    pl.when(active & (step % 2 == 1))(lambda: stages(act_b, act_a))


def _conv_ffn(x2d, g, w_up, conv_w, conv_b, w_down, *, layer, seq, tm=1024, fc=512):
    t, d = x2d.shape
    f = w_down.shape[1]
    nf = f // fc
    n_tiles = t // tm
    assert t % tm == 0 and seq % tm == 0 and f % fc == 0
    kernel = functools.partial(_ffn_kernel, tiles_per_seq=seq // tm)

    def up_chunk(i, j):
        return jnp.where(i < n_tiles, j, nf - 1)

    def down_chunk(i, j):
        return jnp.where((i < n_tiles) & (j > 0), j - 1, nf - 1)

    def down_tile(i, j):
        return jnp.clip(jnp.where(j == 0, i - 1, i), 0, n_tiles - 1)

    return pl.pallas_call(
        kernel,
        grid=(n_tiles + 1, nf),
        in_specs=[
            pl.BlockSpec((tm, d), lambda i, j: (jnp.minimum(i, n_tiles - 1), 0)),
            pl.BlockSpec((1, d), lambda i, j: (0, 0)),
            pl.BlockSpec((None, d, fc), lambda i, j: (layer, 0, up_chunk(i, j))),
            pl.BlockSpec((None, d, fc), lambda i, j: (layer, 0, nf + up_chunk(i, j))),
            pl.BlockSpec((None, CONV_WIDTH, fc), lambda i, j: (layer, 0, up_chunk(i, j))),
            pl.BlockSpec((None, CONV_WIDTH, fc), lambda i, j: (layer, 0, nf + up_chunk(i, j))),
            pl.BlockSpec((None, 1, fc), lambda i, j: (layer, 0, up_chunk(i, j))),
            pl.BlockSpec((None, 1, fc), lambda i, j: (layer, 0, nf + up_chunk(i, j))),
            pl.BlockSpec((None, fc, d), lambda i, j: (layer, down_chunk(i, j), 0)),
        ],
        out_specs=pl.BlockSpec((tm, d), lambda i, j: (down_tile(i, j), 0)),
        out_shape=jax.ShapeDtypeStruct((t, d), F32),
        scratch_shapes=[
            pltpu.VMEM((tm, d), BF16),
            pltpu.VMEM((tm, fc), BF16),
            pltpu.VMEM((tm, fc), BF16),
            pltpu.VMEM((nf, SCAN_ROWS, fc), F32),
            pltpu.VMEM((nf, SCAN_ROWS, fc), F32),
        ],
        compiler_params=_params("arbitrary", "arbitrary"),
        name="conv_ffn",
    )(x2d, g.reshape(1, d), w_up, w_up, conv_w, conv_w, conv_b, conv_b, w_down)


def _pool_kernel(x_ref, halo_ref, g_ref, w_ref, b_ref, s_ref, o_ref, hbuf, *, tiles_per_seq):
    i = pl.program_id(0)
    tm, d = x_ref.shape
    dg = d // len(POOL_WINDOWS)
    x = x_ref[...]
    g = g_ref[...]
    h = _rms(x, g)
    tile_in_seq = i % tiles_per_seq
    halo = jnp.where(tile_in_seq == 0, 0.0, _rms(halo_ref[...], g))
    hbuf[0:POOL_HALO, :] = halo
    hbuf[POOL_HALO:, :] = h
    t_in_seq = tile_in_seq * tm + lax.broadcasted_iota(jnp.int32, (tm, dg), 0)
    for gi, win in enumerate(POOL_WINDOWS):
        cols = slice(gi * dg, (gi + 1) * dg)
        hg = h[:, cols]
        acc = hg
        for lag in range(1, win):
            acc = acc + hbuf[POOL_HALO - lag:POOL_HALO - lag + tm, cols]
        cnt = jnp.minimum(t_in_seq + 1, win).astype(F32)
        pooled = acc / cnt - hg
        y = _dot(pooled.astype(BF16), w_ref[gi])
        o_ref[:, cols] = x[:, cols] + (y + b_ref[:, cols]) * s_ref[:, cols]


def _pool_mixer(x2d, g, w, b, scale, *, seq, tm=512):
    t, d = x2d.shape
    ng, dg, _ = w.shape
    assert t % tm == 0 and seq % tm == 0 and tm % POOL_HALO == 0
    halo_blocks_per_tile = tm // POOL_HALO
    kernel = functools.partial(_pool_kernel, tiles_per_seq=seq // tm)
    return pl.pallas_call(
        kernel,
        grid=(t // tm,),
        in_specs=[
            pl.BlockSpec((tm, d), lambda i: (i, 0)),
            pl.BlockSpec((POOL_HALO, d), lambda i: (jnp.maximum(i * halo_blocks_per_tile - 1, 0), 0)),
            pl.BlockSpec((1, d), lambda i: (0, 0)),
            pl.BlockSpec((ng, dg, dg), lambda i: (0, 0, 0)),
            pl.BlockSpec((1, d), lambda i: (0, 0)),
            pl.BlockSpec((1, d), lambda i: (0, 0)),
        ],
        out_specs=pl.BlockSpec((tm, d), lambda i: (i, 0)),
        out_shape=jax.ShapeDtypeStruct((t, d), F32),
        scratch_shapes=[pltpu.VMEM((POOL_HALO + tm, d), F32)],
        compiler_params=_params("arbitrary"),
        name="pool_mixer",
    )(x2d, x2d, g.reshape(1, d), w, b.reshape(1, d), scale.reshape(1, d))


def _qkv_kernel(x_ref, g_ref, w_ref, gain_ref, o_ref, h_scr, *, norm_tiles, sub):
    j = pl.program_id(1)
    tn = w_ref.shape[1]

    @pl.when(j == 0)
    def _():
        h_scr[...] = _rms(x_ref[...], g_ref[...]).astype(BF16)

    h = h_scr[...]
    normed = j < norm_tiles
    accs = [_dot(h, w_ref[:, c * sub:(c + 1) * sub]) for c in range(tn // sub)]
    for c, acc in enumerate(accs):
        for hh in range(sub // SB_HEAD_DIM):
            head = acc[:, hh * SB_HEAD_DIM:(hh + 1) * SB_HEAD_DIM]
            cols = slice(c * sub + hh * SB_HEAD_DIM, c * sub + (hh + 1) * SB_HEAD_DIM)
            o_ref[:, cols] = jnp.where(normed, _rms(head, gain_ref[:, cols]), head).astype(BF16)


def _qkv_proj(x2d, g, w_qkv, gain_row, *, tm=1024, tn=1024, sub=512):
    t, d = x2d.shape
    n = w_qkv.shape[1]
    assert t % tm == 0 and n % tn == 0 and (2 * d) % tn == 0 and sub % SB_HEAD_DIM == 0
    assert tn % sub == 0
    kernel = functools.partial(_qkv_kernel, norm_tiles=2 * d // tn, sub=sub)
    return pl.pallas_call(
        kernel,
        grid=(t // tm, n // tn),
        in_specs=[
            pl.BlockSpec((tm, d), lambda i, j: (i, 0)),
            pl.BlockSpec((1, d), lambda i, j: (0, 0)),
            pl.BlockSpec((d, tn), lambda i, j: (0, j)),
            pl.BlockSpec((1, tn), lambda i, j: (0, j)),
        ],
        out_specs=pl.BlockSpec((tm, tn), lambda i, j: (i, j)),
        out_shape=jax.ShapeDtypeStruct((t, n), BF16),
        scratch_shapes=[pltpu.VMEM((tm, d), BF16)],
        compiler_params=_params("arbitrary", "arbitrary"),
        name="sb_qkv",
    )(x2d, g.reshape(1, d), w_qkv, gain_row)


def _sb_attn_kernel(q_ref, k_ref, v_ref, o_ref, *, chain_rows):
    qi = pl.program_id(2)
    blk = q_ref.shape[0]
    dh = SB_HEAD_DIM
    n_heads = q_ref.shape[1] // dh
    row = lax.broadcasted_iota(jnp.int32, (blk, blk), 0)
    col = lax.broadcasted_iota(jnp.int32, (blk, blk), 1)
    later = jnp.where(row > col, 1.0, 0.0).astype(BF16)
    later2 = jnp.concatenate([later, later], axis=0)
    row_g = lax.broadcasted_iota(jnp.int32, (chain_rows, blk), 0)
    col_g = lax.broadcasted_iota(jnp.int32, (chain_rows, blk), 1)

    chains = [(hh, r0) for hh in range(n_heads) for r0 in range(0, blk, chain_rows)]

    def key_block(kb, state, diagonal):
        start = pl.multiple_of(kb * blk, blk)
        cols = [slice(hh * dh, (hh + 1) * dh) for hh, _ in chains]
        rows = [slice(r0, r0 + chain_rows) for _, r0 in chains]
        zs = [lax.dot_general(q_ref[r, c], k_ref[pl.ds(start, blk), c], (((1,), (1,)), ((), ())),
                              preferred_element_type=F32) for r, c in zip(rows, cols)]
        log_betas, log_1m_betas, splits = [], [], []
        for (_, r0), z in zip(chains, zs):
            log_beta = jnp.minimum(z, 0.0) - jnp.log(1.0 + jnp.exp(-jnp.abs(z)))
            log_1m_beta = log_beta - z
            if diagonal:
                log_1m_beta = jnp.where(col_g < row_g + r0, log_1m_beta, 0.0)
            hi = log_1m_beta.astype(BF16)
            lo = (log_1m_beta - hi.astype(F32)).astype(BF16)
            log_betas.append(log_beta)
            log_1m_betas.append(log_1m_beta)
            splits.append(jnp.concatenate([hi, lo], axis=1))
        remains = [_dot(s, later2) for s in splits]
        ws = []
        for (_, r0), log_beta, remain, (carry, _) in zip(chains, log_betas, remains, state):
            w = jnp.exp(log_beta + remain + carry)
            if diagonal:
                w = jnp.where(col_g < row_g + r0, w, 0.0)
            ws.append(w.astype(BF16))
        out = []
        for c, w, log_1m_beta, (carry, acc) in zip(cols, ws, log_1m_betas, state):
            out.append((carry + jnp.sum(log_1m_beta, axis=-1, keepdims=True),
                        acc + _dot(w, v_ref[pl.ds(start, blk), c])))
        return tuple(out)

    zero = (jnp.zeros((chain_rows, 1), F32), jnp.zeros((chain_rows, dh), F32))
    state = key_block(qi, (zero,) * len(chains), True)
    state = lax.fori_loop(0, qi, lambda n, st: key_block(qi - 1 - n, st, False), state)
    for (hh, r0), (_, acc) in zip(chains, state):
        o_ref[r0:r0 + chain_rows, hh * dh:(hh + 1) * dh] = acc.astype(BF16)


def _sb_attention(qkv3d, *, blk=256, heads_per_step=4, chain_rows=256):
    bsz, seq, n3 = qkv3d.shape
    d = n3 // 3
    wide = heads_per_step * SB_HEAD_DIM
    groups = d // wide
    assert seq % blk == 0 and d % wide == 0 and blk % chain_rows == 0
    return pl.pallas_call(
        functools.partial(_sb_attn_kernel, chain_rows=chain_rows),
        grid=(bsz, groups, seq // blk),
        in_specs=[
            pl.BlockSpec((None, blk, wide), lambda b, h, qi: (b, qi, h)),
            pl.BlockSpec((None, seq, wide), lambda b, h, qi: (b, 0, groups + h)),
            pl.BlockSpec((None, seq, wide), lambda b, h, qi: (b, 0, 2 * groups + h)),
        ],
        out_specs=pl.BlockSpec((None, blk, wide), lambda b, h, qi: (b, qi, h)),
        out_shape=jax.ShapeDtypeStruct((bsz, seq, d), BF16),
        compiler_params=_params("arbitrary", "arbitrary", "arbitrary"),
        name="sb_attention",
    )(qkv3d, qkv3d, qkv3d)


def _proj_res_kernel(a_ref, w_ref, r_ref, o_ref, *, sub):
    a = a_ref[...]
    tn = w_ref.shape[1]
    accs = [_dot(a, w_ref[:, c * sub:(c + 1) * sub]) for c in range(tn // sub)]
    for c, acc in enumerate(accs):
        cols = slice(c * sub, (c + 1) * sub)
        o_ref[:, cols] = r_ref[:, cols] + acc


def _proj_residual(a2d, w, res2d, *, tm=1024, tn=1024, sub=512):
    t, k = a2d.shape
    n = w.shape[1]
    assert t % tm == 0 and n % tn == 0 and tn % sub == 0
    return pl.pallas_call(
        functools.partial(_proj_res_kernel, sub=sub),
        grid=(t // tm, n // tn),
        in_specs=[
            pl.BlockSpec((tm, k), lambda i, j: (i, 0)),
            pl.BlockSpec((k, tn), lambda i, j: (0, j)),
            pl.BlockSpec((tm, tn), lambda i, j: (i, j)),
        ],
        out_specs=pl.BlockSpec((tm, tn), lambda i, j: (i, j)),
        out_shape=jax.ShapeDtypeStruct((t, n), F32),
        compiler_params=_params("arbitrary", "arbitrary"),
        name="proj_residual",
    )(a2d, w, res2d)


def _rmsnorm_kernel(x_ref, g_ref, o_ref):
    o_ref[...] = _rms(x_ref[...], g_ref[...])


def _rmsnorm(x2d, g, *, tm=512):
    t, d = x2d.shape
    assert t % tm == 0
    return pl.pallas_call(
        _rmsnorm_kernel,
        grid=(t // tm,),
        in_specs=[pl.BlockSpec((tm, d), lambda i: (i, 0)), pl.BlockSpec((1, d), lambda i: (0, 0))],
        out_specs=pl.BlockSpec((tm, d), lambda i: (i, 0)),
        out_shape=jax.ShapeDtypeStruct((t, d), F32),
        compiler_params=_params("arbitrary"),
        name="rmsnorm",
    )(x2d, g.reshape(1, d))


def _s5_kernel(u_ref, wb_ref, tab_ref, pw_ref, wc_ref, dskip_ref, o_ref,
               up_scr, st_scr, xs_scr, y_scr, carry_scr):
    tc = u_ref.shape[0]
    ns = wb_ref.shape[1] // 2
    n_slabs = tc // SCAN_ROWS

    planes = up_scr.shape[0]
    for r in range(SCAN_ROWS):
        for c in range(planes):
            up_scr[c, pl.ds(r, n_slabs, stride=SCAN_ROWS), :] = (
                u_ref[r * n_slabs:(r + 1) * n_slabs, c * LANES:(c + 1) * LANES])
    u_perm = jnp.concatenate([up_scr[c] for c in range(planes)], axis=1)
    st_scr[...] = _dot(u_perm.astype(BF16), wb_ref[...])

    @pl.when(pl.program_id(2) == 0)
    def _():
        carry_scr[...] = jnp.zeros(carry_scr.shape, F32)

    ar = tab_ref[0]
    ai = tab_ref[1]

    def slab(s, x):
        xr, xi = x
        r0 = pl.multiple_of(s * SCAN_ROWS, SCAN_ROWS)
        xr, xi = (ar * xr - ai * xi + st_scr[pl.ds(r0, SCAN_ROWS), 0:ns],
                  ar * xi + ai * xr + st_scr[pl.ds(r0, SCAN_ROWS), ns:2 * ns])
        st_scr[pl.ds(r0, SCAN_ROWS), 0:ns] = xr
        st_scr[pl.ds(r0, SCAN_ROWS), ns:2 * ns] = xi
        return xr, xi

    zero = jnp.zeros((SCAN_ROWS, ns), F32)
    yr, yi = lax.fori_loop(0, n_slabs, slab, (zero, zero), unroll=2)

    cr = carry_scr[0]
    ci = carry_scr[1]
    for step, lag in enumerate((1, 2, 4)):
        mr = tab_ref[2 + 2 * step]
        mi = tab_ref[3 + 2 * step]
        sr = pltpu.roll(yr, lag, axis=0)
        si = pltpu.roll(yi, lag, axis=0)
        yr, yi = yr + mr * sr - mi * si, yi + mr * si + mi * sr
    pr = tab_ref[8]
    pi = tab_ref[9]
    yr, yi = yr + pr * cr - pi * ci, yi + pr * ci + pi * cr
    first = lax.broadcasted_iota(jnp.int32, (SCAN_ROWS, ns), 0) == 0
    init_r = jnp.where(first, cr, pltpu.roll(yr, 1, axis=0))
    init_i = jnp.where(first, ci, pltpu.roll(yi, 1, axis=0))
    last = SCAN_ROWS - 1
    carry_scr[0] = jnp.broadcast_to(yr[last:, :], (SCAN_ROWS, ns))
    carry_scr[1] = jnp.broadcast_to(yi[last:, :], (SCAN_ROWS, ns))

    def fix(s2, _):
        xr, xi = [], []
        for k in range(2):
            s = 2 * s2 + k
            r0 = pl.multiple_of(s * SCAN_ROWS, SCAN_ROWS)
            qr = pw_ref[0, pl.ds(s, 1), :]
            qi = pw_ref[1, pl.ds(s, 1), :]
            xr.append(st_scr[pl.ds(r0, SCAN_ROWS), 0:ns] + qr * init_r - qi * init_i)
            xi.append(st_scr[pl.ds(r0, SCAN_ROWS), ns:2 * ns] + qr * init_i + qi * init_r)
        r2 = pl.multiple_of(s2 * 2 * SCAN_ROWS, 2 * SCAN_ROWS)
        xs_scr[pl.ds(r2, 2 * SCAN_ROWS), 0:ns] = jnp.concatenate(xr, axis=0).astype(BF16)
        xs_scr[pl.ds(r2, 2 * SCAN_ROWS), ns:2 * ns] = jnp.concatenate(xi, axis=0).astype(BF16)
        return 0

    lax.fori_loop(0, n_slabs // 2, fix, 0)

    y_perm = _dot(xs_scr[...], wc_ref[...])
    for c in range(planes):
        y_scr[c] = y_perm[:, c * LANES:(c + 1) * LANES]
    for r in range(SCAN_ROWS):
        rows = slice(r * n_slabs, (r + 1) * n_slabs)
        for c in range(planes):
            cols = slice(c * LANES, (c + 1) * LANES)
            y = y_scr[c, pl.ds(r, n_slabs, stride=SCAN_ROWS), :] + dskip_ref[:, cols] * u_ref[rows, cols]
            o_ref[rows, cols] = jax.nn.gelu(y, approximate=True).astype(BF16)


def _s5_core(h2d, wb, tab, pw, wc, d_skip, *, bsz, seq):
    t, d = h2d.shape
    nblk, cb, ns2 = wb.shape
    n_slabs = pw.shape[2]
    tc = n_slabs * SCAN_ROWS
    assert seq % tc == 0 and n_slabs % 2 == 0 and nblk * cb == d
    nt = seq // tc
    return pl.pallas_call(
        _s5_kernel,
        grid=(bsz, nblk, nt),
        in_specs=[
            pl.BlockSpec((tc, cb), lambda b, gb, tt: (b * nt + tt, gb)),
            pl.BlockSpec((None, cb, ns2), lambda b, gb, tt: (gb, 0, 0)),
            pl.BlockSpec((None, 10, SCAN_ROWS, ns2 // 2), lambda b, gb, tt: (gb, 0, 0, 0)),
            pl.BlockSpec((None, 2, n_slabs, ns2 // 2), lambda b, gb, tt: (gb, 0, 0, 0)),
            pl.BlockSpec((None, ns2, cb), lambda b, gb, tt: (gb, 0, 0)),
            pl.BlockSpec((1, cb), lambda b, gb, tt: (0, gb)),
        ],
        out_specs=pl.BlockSpec((tc, cb), lambda b, gb, tt: (b * nt + tt, gb)),
        out_shape=jax.ShapeDtypeStruct((t, d), BF16),
        scratch_shapes=[
            pltpu.VMEM((cb // LANES, tc, LANES), F32),
            pltpu.VMEM((tc, ns2), F32),
            pltpu.VMEM((tc, ns2), BF16),
            pltpu.VMEM((cb // LANES, tc, LANES), F32),
            pltpu.VMEM((2, SCAN_ROWS, ns2 // 2), F32),
        ],
        compiler_params=_params("arbitrary", "arbitrary", "arbitrary"),
        name="s5_scan",
    )(h2d, wb, tab, pw, wc, d_skip.reshape(1, d))


def _glu_res_kernel(a_ref, wv_ref, wg_ref, bv_ref, bg_ref, r_ref, o_ref, *, sub):
    a = a_ref[...]
    tn = wv_ref.shape[1]
    pairs = []
    for c in range(tn // sub):
        cols = slice(c * sub, (c + 1) * sub)
        pairs.append((cols, _dot(a, wv_ref[:, cols]), _dot(a, wg_ref[:, cols])))
    for cols, val, gate in pairs:
        o_ref[:, cols] = r_ref[:, cols] + (val + bv_ref[:, cols]) * jax.nn.sigmoid(gate + bg_ref[:, cols])


def _glu_residual(a2d, w, b, res2d, *, tm=1024, tn=512, sub=256):
    t, k = a2d.shape
    n = w.shape[1] // 2
    assert t % tm == 0 and n % tn == 0 and tn % sub == 0
    nj = n // tn
    return pl.pallas_call(
        functools.partial(_glu_res_kernel, sub=sub),
        grid=(t // tm, nj),
        in_specs=[
            pl.BlockSpec((tm, k), lambda i, j: (i, 0)),
            pl.BlockSpec((k, tn), lambda i, j: (0, j)),
            pl.BlockSpec((k, tn), lambda i, j: (0, nj + j)),
            pl.BlockSpec((1, tn), lambda i, j: (0, j)),
            pl.BlockSpec((1, tn), lambda i, j: (0, nj + j)),
            pl.BlockSpec((tm, tn), lambda i, j: (i, j)),
        ],
        out_specs=pl.BlockSpec((tm, tn), lambda i, j: (i, j)),
        out_shape=jax.ShapeDtypeStruct((t, n), F32),
        compiler_params=_params("arbitrary", "arbitrary"),
        name="glu_residual",
    )(a2d, w, w, b.reshape(1, -1), b.reshape(1, -1), res2d)


def _s5_tables(lam_re, lam_im, log_step, b_re, b_im, c_re, c_im, *, n_slabs):
    groups, p = lam_re.shape
    hc = b_re.shape[2]
    gpb = SSM_GROUPS_PER_BLOCK
    nblk = groups // gpb
    step = jnp.exp(log_step)[:, None]
    mag = jnp.exp(lam_re * step)
    a_re = mag * jnp.cos(lam_im * step)
    a_im = mag * jnp.sin(lam_im * step)
    den = lam_re * lam_re + lam_im * lam_im
    f_re = ((a_re - 1.0) * lam_re + a_im * lam_im) / den
    f_im = (a_im * lam_re - (a_re - 1.0) * lam_im) / den
    bb_re = f_re[..., None] * b_re - f_im[..., None] * b_im
    bb_im = f_re[..., None] * b_im + f_im[..., None] * b_re
    eye = jnp.eye(gpb, dtype=F32)

    def block_diag_in(bb):
        bb = bb.reshape(nblk, gpb, p, hc).transpose(0, 1, 3, 2)
        return jnp.einsum('bghp,gk->bghkp', bb, eye).reshape(nblk, gpb * hc, gpb * p)

    def block_diag_out(c):
        c = c.reshape(nblk, gpb, hc, p).transpose(0, 1, 3, 2)
        return jnp.einsum('bgpo,gk->bgpko', c, eye).reshape(nblk, gpb * p, gpb * hc)

    wb = jnp.concatenate([block_diag_in(bb_re), block_diag_in(bb_im)], axis=2).astype(BF16)
    wc = jnp.concatenate([block_diag_out(c_re), -block_diag_out(c_im)], axis=1).astype(BF16)

    def powers(zr, zi, count):
        pr, pi = zr[:, None, :], zi[:, None, :]
        while pr.shape[1] < count:
            tr, ti = pr[:, -1:, :], pi[:, -1:, :]
            pr, pi = (jnp.concatenate([pr, pr * tr - pi * ti], axis=1),
                      jnp.concatenate([pi, pr * ti + pi * tr], axis=1))
        return pr[:, :count], pi[:, :count]

    ar = a_re.reshape(nblk, gpb * p)
    ai = a_im.reshape(nblk, gpb * p)
    pw_re, pw_im = powers(ar, ai, n_slabs)
    pw = jnp.stack([pw_re, pw_im], axis=1)
    lr, li = powers(pw_re[:, -1], pw_im[:, -1], SCAN_ROWS)
    rows = jnp.arange(SCAN_ROWS)[None, :, None]
    ones = jnp.ones((1, SCAN_ROWS, 1), F32)
    tabs = [ar[:, None, :] * ones, ai[:, None, :] * ones]
    for lag in (1, 2, 4):
        for z in (lr[:, lag - 1], li[:, lag - 1]):
            tabs.append(jnp.where(rows >= lag, z[:, None, :], 0.0))
    tabs += [lr, li]
    tab = jnp.stack(tabs, axis=1)
    return wb, tab, pw, wc


def kernel(x, norm_mix_g, norm_ffn_g, pool_w, pool_b, pool_scale, sb_w_qkv, sb_q_gain, sb_k_gain, sb_w_o, ssm_lam_re, ssm_lam_im, ssm_log_step, ssm_b_re, ssm_b_im, ssm_c_re, ssm_c_im, ssm_d, ssm_w_glu, ssm_b_glu, ffn_w_up, ffn_conv_w, ffn_conv_b, ffn_w_down):
    bsz, seq, d = x.shape
    depth = norm_mix_g.shape[0]
    heads = d // SB_HEAD_DIM
    x2d = x.reshape(bsz * seq, d)
    ffn_w_up = ffn_w_up.astype(BF16)
    ffn_w_down = ffn_w_down.astype(BF16)
    ffn_conv_b = ffn_conv_b.reshape(depth, 1, -1)
    for i in range(depth):
        kind = i % 3
        j = i // 3
        if kind == 0:
            x2d = _pool_mixer(x2d, norm_mix_g[i], pool_w[j].astype(BF16), pool_b[j], pool_scale[j],
                              seq=seq)
        elif kind == 1:
            gain_row = jnp.concatenate([
                jnp.tile(sb_q_gain[j] * (1.0 / math.sqrt(SB_HEAD_DIM)), heads),
                jnp.tile(sb_k_gain[j], heads),
                jnp.ones((d,), F32)]).reshape(1, 3 * d)
            qkv = _qkv_proj(x2d, norm_mix_g[i], sb_w_qkv[j].astype(BF16), gain_row)
            o = _sb_attention(qkv.reshape(bsz, seq, 3 * d))
            x2d = _proj_residual(o.reshape(bsz * seq, d), sb_w_o[j].astype(BF16), x2d)
        else:
            wb, tab, pw, wc = _s5_tables(ssm_lam_re[j], ssm_lam_im[j], ssm_log_step[j],
                                         ssm_b_re[j], ssm_b_im[j], ssm_c_re[j], ssm_c_im[j],
                                         n_slabs=S5_TIME_CHUNK // SCAN_ROWS)
            h = _rmsnorm(x2d, norm_mix_g[i])
            y = _s5_core(h, wb, tab, pw, wc, ssm_d[j], bsz=bsz, seq=seq)
            x2d = _glu_residual(y, ssm_w_glu[j].astype(BF16), ssm_b_glu[j], x2d)
        x2d = _conv_ffn(x2d, norm_ffn_g[i], ffn_w_up, ffn_conv_w, ffn_conv_b, ffn_w_down,
                        layer=i, seq=seq)
    return x2d.reshape(bsz, seq, d)
```

```python
import functools
import math

import jax
import jax.numpy as jnp
from jax import lax
from jax.experimental import pallas as pl
from jax.experimental.pallas import tpu as pltpu

RMS_EPS = 1e-6
POOL_WINDOWS = (2, 4, 8, 16)
POOL_HALO = 32
SB_HEAD_DIM = 128
SSM_GROUP_CH = 16
SSM_STATE = 64
SSM_GROUPS_PER_BLOCK = 16
SCAN_ROWS = 8
S5_TIME_CHUNK = 1024
LANES = 128
CONV_WIDTH = 3

V7X_VMEM_LIMIT_BYTES = 60 * 1024 * 1024

F32 = jnp.float32
BF16 = jnp.bfloat16


def _params(*semantics):
    return pltpu.CompilerParams(dimension_semantics=semantics,
                                vmem_limit_bytes=V7X_VMEM_LIMIT_BYTES)


def _rms(x, g):
    ms = jnp.mean(x * x, axis=-1, keepdims=True)
    return x * lax.rsqrt(ms + RMS_EPS) * g


def _dot(a, b):
    return jnp.dot(a, b, preferred_element_type=F32)


def _ffn_kernel(x_ref, g_ref, wv_ref, wg_ref, cwv_ref, cwg_ref, cbv_ref, cbg_ref, wd_ref,
                o_ref, h_scr, act_a, act_b, tail_v, tail_g, *, tiles_per_seq):
    i = pl.program_id(0)
    j = pl.program_id(1)
    n_tiles = pl.num_programs(0) - 1
    nf = pl.num_programs(1)
    tm = x_ref.shape[0]
    fc = wv_ref.shape[1]
    step = i * nf + j

    @pl.when(step == 0)
    def _():
        act_b[...] = jnp.zeros(act_b.shape, BF16)
        o_ref[...] = jnp.zeros(o_ref.shape, F32)

    @pl.when((j == 0) & (i < n_tiles))
    def _():
        h_scr[...] = _rms(x_ref[...], g_ref[...]).astype(BF16)

        @pl.when(i % tiles_per_seq == 0)
        def _():
            tail_v[...] = jnp.zeros(tail_v.shape, F32)
            tail_g[...] = jnp.zeros(tail_g.shape, F32)

    @pl.when((j == 1) & (i < n_tiles))
    def _():
        o_ref[...] = x_ref[...]

    chunk = jnp.where(i < n_tiles, j, nf - 1)
    active = (i < n_tiles) | (j == 0)
    row = lax.broadcasted_iota(jnp.int32, (SCAN_ROWS, fc), 0)

    def causal_conv(up, tail_ref, cw_ref, cb_ref):
        prev = tail_ref[chunk]
        tail_ref[chunk] = up[tm - SCAN_ROWS:, :]
        out = cb_ref[...] + cw_ref[2:3, :] * up
        for lag in (1, 2):
            shifted = pltpu.roll(up, lag, axis=0)
            head = jnp.where(row < lag, pltpu.roll(prev, lag, axis=0), shifted[:SCAN_ROWS])
            shifted = jnp.concatenate([head, shifted[SCAN_ROWS:]], axis=0)
            out = out + cw_ref[2 - lag:3 - lag, :] * shifted
        return out

    def stages(act_new, act_old):
        h = h_scr[...]
        val = causal_conv(_dot(h, wv_ref[...]), tail_v, cwv_ref, cbv_ref)
        up_gate = _dot(h, wg_ref[...])
        o_ref[...] += _dot(act_old[...], wd_ref[...])
        gate = causal_conv(up_gate, tail_g, cwg_ref, cbg_ref)
        half = 0.5 * gate
        act_new[...] = ((half + half * jnp.tanh(half)) * val).astype(BF16)

    pl.when(active & (step % 2 == 0))(lambda: stages(act_a, act_b))
    pl.when(active & (step % 2 == 1))(lambda: stages(act_b, act_a))


def _conv_ffn(x2d, g, w_up, conv_w, conv_b, w_down, *, layer, seq, tm=1024, fc=512):
    t, d = x2d.shape
    f = w_down.shape[1]
    nf = f // fc
    n_tiles = t // tm
    assert t % tm == 0 and seq % tm == 0 and f % fc == 0
    kernel = functools.partial(_ffn_kernel, tiles_per_seq=seq // tm)

    def up_chunk(i, j):
        return jnp.where(i < n_tiles, j, nf - 1)

    def down_chunk(i, j):
        return jnp.where((i < n_tiles) & (j > 0), j - 1, nf - 1)

    def down_tile(i, j):
        return jnp.clip(jnp.where(j == 0, i - 1, i), 0, n_tiles - 1)

    return pl.pallas_call(
        kernel,
        grid=(n_tiles + 1, nf),
        in_specs=[
            pl.BlockSpec((tm, d), lambda i, j: (jnp.minimum(i, n_tiles - 1), 0)),
            pl.BlockSpec((1, d), lambda i, j: (0, 0)),
            pl.BlockSpec((None, d, fc), lambda i, j: (layer, 0, up_chunk(i, j))),
            pl.BlockSpec((None, d, fc), lambda i, j: (layer, 0, nf + up_chunk(i, j))),
            pl.BlockSpec((None, CONV_WIDTH, fc), lambda i, j: (layer, 0, up_chunk(i, j))),
            pl.BlockSpec((None, CONV_WIDTH, fc), lambda i, j: (layer, 0, nf + up_chunk(i, j))),
            pl.BlockSpec((None, 1, fc), lambda i, j: (layer, 0, up_chunk(i, j))),
            pl.BlockSpec((None, 1, fc), lambda i, j: (layer, 0, nf + up_chunk(i, j))),
            pl.BlockSpec((None, fc, d), lambda i, j: (layer, down_chunk(i, j), 0)),
        ],
        out_specs=pl.BlockSpec((tm, d), lambda i, j: (down_tile(i, j), 0)),
        out_shape=jax.ShapeDtypeStruct((t, d), F32),
        scratch_shapes=[
            pltpu.VMEM((tm, d), BF16),
            pltpu.VMEM((tm, fc), BF16),
            pltpu.VMEM((tm, fc), BF16),
            pltpu.VMEM((nf, SCAN_ROWS, fc), F32),
            pltpu.VMEM((nf, SCAN_ROWS, fc), F32),
        ],
        compiler_params=_params("arbitrary", "arbitrary"),
        name="conv_ffn",
    )(x2d, g.reshape(1, d), w_up, w_up, conv_w, conv_w, conv_b, conv_b, w_down)


def _pool_kernel(x_ref, halo_ref, g_ref, w_ref, b_ref, s_ref, o_ref, hbuf, sum_a, sum_b,
                 *, tiles_per_seq):
    i = pl.program_id(0)
    tm, d = x_ref.shape
    dg = d // len(POOL_WINDOWS)
    x = x_ref[...]
    g = g_ref[...]
    h = _rms(x, g)
    tile_in_seq = i % tiles_per_seq
    halo = jnp.where(tile_in_seq == 0, 0.0, _rms(halo_ref[...], g))
    hbuf[0:POOL_HALO, :] = halo
    hbuf[POOL_HALO:, :] = h
    t_in_seq = tile_in_seq * tm + lax.broadcasted_iota(jnp.int32, (tm, dg), 0)
    for gi, win in enumerate(POOL_WINDOWS):
        cols = slice(gi * dg, (gi + 1) * dg)
        levels = win.bit_length() - 1
        src, src_cols = hbuf, cols
        for k in range(1, levels + 1):
            lag = 1 << (k - 1)
            start = POOL_HALO - SCAN_ROWS * (levels - k)
            rows = POOL_HALO + tm - start
            acc = src[start:start + rows, src_cols] + src[start - lag:start - lag + rows, src_cols]
            if k < levels:
                src, src_cols = (sum_a, sum_b)[k % 2], slice(0, dg)
                src[start:start + rows, :] = acc
        cnt = jnp.minimum(t_in_seq + 1, win).astype(F32)
        pooled = acc / cnt - h[:, cols]
        y = _dot(pooled.astype(BF16), w_ref[gi])
        o_ref[:, cols] = x[:, cols] + (y + b_ref[:, cols]) * s_ref[:, cols]


def _pool_mixer(x2d, g, w, b, scale, *, seq, tm=512):
    t, d = x2d.shape
    ng, dg, _ = w.shape
    assert t % tm == 0 and seq % tm == 0 and tm % POOL_HALO == 0
    halo_blocks_per_tile = tm // POOL_HALO
    kernel = functools.partial(_pool_kernel, tiles_per_seq=seq // tm)
    return pl.pallas_call(
        kernel,
        grid=(t // tm,),
        in_specs=[
            pl.BlockSpec((tm, d), lambda i: (i, 0)),
            pl.BlockSpec((POOL_HALO, d), lambda i: (jnp.maximum(i * halo_blocks_per_tile - 1, 0), 0)),
            pl.BlockSpec((1, d), lambda i: (0, 0)),
            pl.BlockSpec((ng, dg, dg), lambda i: (0, 0, 0)),
            pl.BlockSpec((1, d), lambda i: (0, 0)),
            pl.BlockSpec((1, d), lambda i: (0, 0)),
        ],
        out_specs=pl.BlockSpec((tm, d), lambda i: (i, 0)),
        out_shape=jax.ShapeDtypeStruct((t, d), F32),
        scratch_shapes=[pltpu.VMEM((POOL_HALO + tm, d), F32),
                        pltpu.VMEM((POOL_HALO + tm, dg), F32),
                        pltpu.VMEM((POOL_HALO + tm, dg), F32)],
        compiler_params=_params("arbitrary"),
        name="pool_mixer",
    )(x2d, x2d, g.reshape(1, d), w, b.reshape(1, d), scale.reshape(1, d))


def _qkv_kernel(x_ref, g_ref, w_ref, gain_ref, o_ref, h_scr, *, norm_tiles, sub):
    j = pl.program_id(1)
    tn = w_ref.shape[1]

    @pl.when(j == 0)
    def _():
        h_scr[...] = _rms(x_ref[...], g_ref[...]).astype(BF16)

    h = h_scr[...]
    normed = j < norm_tiles
    accs = [_dot(h, w_ref[:, c * sub:(c + 1) * sub]) for c in range(tn // sub)]
    for c, acc in enumerate(accs):
        for hh in range(sub // SB_HEAD_DIM):
            head = acc[:, hh * SB_HEAD_DIM:(hh + 1) * SB_HEAD_DIM]
            cols = slice(c * sub + hh * SB_HEAD_DIM, c * sub + (hh + 1) * SB_HEAD_DIM)
            o_ref[:, cols] = jnp.where(normed, _rms(head, gain_ref[:, cols]), head).astype(BF16)


def _qkv_proj(x2d, g, w_qkv, gain_row, *, tm=1024, tn=1024, sub=512):
    t, d = x2d.shape
    n = w_qkv.shape[1]
    assert t % tm == 0 and n % tn == 0 and (2 * d) % tn == 0 and sub % SB_HEAD_DIM == 0
    assert tn % sub == 0
    kernel = functools.partial(_qkv_kernel, norm_tiles=2 * d // tn, sub=sub)
    return pl.pallas_call(
        kernel,
        grid=(t // tm, n // tn),
        in_specs=[
            pl.BlockSpec((tm, d), lambda i, j: (i, 0)),
            pl.BlockSpec((1, d), lambda i, j: (0, 0)),
            pl.BlockSpec((d, tn), lambda i, j: (0, j)),
            pl.BlockSpec((1, tn), lambda i, j: (0, j)),
        ],
        out_specs=pl.BlockSpec((tm, tn), lambda i, j: (i, j)),
        out_shape=jax.ShapeDtypeStruct((t, n), BF16),
        scratch_shapes=[pltpu.VMEM((tm, d), BF16)],
        compiler_params=_params("arbitrary", "arbitrary"),
        name="sb_qkv",
    )(x2d, g.reshape(1, d), w_qkv, gain_row)


def _sb_attn_kernel(q_ref, k_ref, v_ref, o_ref, *, chain_rows):
    qi = pl.program_id(2)
    blk = q_ref.shape[0]
    dh = SB_HEAD_DIM
    n_heads = q_ref.shape[1] // dh
    row = lax.broadcasted_iota(jnp.int32, (blk, blk), 0)
    col = lax.broadcasted_iota(jnp.int32, (blk, blk), 1)
    later = jnp.where(row > col, 1.0, 0.0).astype(BF16)
    later2 = jnp.concatenate([later, later], axis=0)
    row_g = lax.broadcasted_iota(jnp.int32, (chain_rows, blk), 0)
    col_g = lax.broadcasted_iota(jnp.int32, (chain_rows, blk), 1)

    chains = [(hh, r0) for hh in range(n_heads) for r0 in range(0, blk, chain_rows)]

    def key_block(kb, state, diagonal):
        start = pl.multiple_of(kb * blk, blk)
        cols = [slice(hh * dh, (hh + 1) * dh) for hh, _ in chains]
        rows = [slice(r0, r0 + chain_rows) for _, r0 in chains]
        zs = [lax.dot_general(q_ref[r, c], k_ref[pl.ds(start, blk), c], (((1,), (1,)), ((), ())),
                              preferred_element_type=F32) for r, c in zip(rows, cols)]
        log_betas, log_1m_betas, splits = [], [], []
        for (_, r0), z in zip(chains, zs):
            log_beta = jnp.minimum(z, 0.0) - jnp.log2(1.0 + jnp.exp2(-jnp.abs(z)))
            log_1m_beta = log_beta - z
            if diagonal:
                log_1m_beta = jnp.where(col_g < row_g + r0, log_1m_beta, 0.0)
            hi = log_1m_beta.astype(BF16)
            lo = (log_1m_beta - hi.astype(F32)).astype(BF16)
            log_betas.append(log_beta)
            log_1m_betas.append(log_1m_beta)
            splits.append(jnp.concatenate([hi, lo], axis=1))
        remains = [_dot(s, later2) for s in splits]
        ws = []
        for (_, r0), log_beta, remain, (carry, _) in zip(chains, log_betas, remains, state):
            w = jnp.exp2(log_beta + remain + carry)
            if diagonal:
                w = jnp.where(col_g < row_g + r0, w, 0.0)
            ws.append(w.astype(BF16))
        out = []
        for c, w, log_1m_beta, (carry, acc) in zip(cols, ws, log_1m_betas, state):
            out.append((carry + jnp.sum(log_1m_beta, axis=-1, keepdims=True),
                        acc + _dot(w, v_ref[pl.ds(start, blk), c])))
        return tuple(out)

    zero = (jnp.zeros((chain_rows, 1), F32), jnp.zeros((chain_rows, dh), F32))
    state = key_block(qi, (zero,) * len(chains), True)
    state = lax.fori_loop(0, qi, lambda n, st: key_block(qi - 1 - n, st, False), state)
    for (hh, r0), (_, acc) in zip(chains, state):
        o_ref[r0:r0 + chain_rows, hh * dh:(hh + 1) * dh] = acc.astype(BF16)


def _sb_attention(qkv3d, *, blk=256, heads_per_step=4, chain_rows=256):
    bsz, seq, n3 = qkv3d.shape
    d = n3 // 3
    wide = heads_per_step * SB_HEAD_DIM
    groups = d // wide
    assert seq % blk == 0 and d % wide == 0 and blk % chain_rows == 0
    return pl.pallas_call(
        functools.partial(_sb_attn_kernel, chain_rows=chain_rows),
        grid=(bsz, groups, seq // blk),
        in_specs=[
            pl.BlockSpec((None, blk, wide), lambda b, h, qi: (b, qi, h)),
            pl.BlockSpec((None, seq, wide), lambda b, h, qi: (b, 0, groups + h)),
            pl.BlockSpec((None, seq, wide), lambda b, h, qi: (b, 0, 2 * groups + h)),
        ],
        out_specs=pl.BlockSpec((None, blk, wide), lambda b, h, qi: (b, qi, h)),
        out_shape=jax.ShapeDtypeStruct((bsz, seq, d), BF16),
        compiler_params=_params("arbitrary", "arbitrary", "arbitrary"),
        name="sb_attention",
    )(qkv3d, qkv3d, qkv3d)


def _proj_res_kernel(a_ref, w_ref, r_ref, o_ref, *, sub):
    a = a_ref[...]
    tn = w_ref.shape[1]
    accs = [_dot(a, w_ref[:, c * sub:(c + 1) * sub]) for c in range(tn // sub)]
    for c, acc in enumerate(accs):
        cols = slice(c * sub, (c + 1) * sub)
        o_ref[:, cols] = r_ref[:, cols] + acc


def _proj_residual(a2d, w, res2d, *, tm=1024, tn=1024, sub=512):
    t, k = a2d.shape
    n = w.shape[1]
    assert t % tm == 0 and n % tn == 0 and tn % sub == 0
    return pl.pallas_call(
        functools.partial(_proj_res_kernel, sub=sub),
        grid=(t // tm, n // tn),
        in_specs=[
            pl.BlockSpec((tm, k), lambda i, j: (i, 0)),
            pl.BlockSpec((k, tn), lambda i, j: (0, j)),
            pl.BlockSpec((tm, tn), lambda i, j: (i, j)),
        ],
        out_specs=pl.BlockSpec((tm, tn), lambda i, j: (i, j)),
        out_shape=jax.ShapeDtypeStruct((t, n), F32),
        compiler_params=_params("arbitrary", "arbitrary"),
        name="proj_residual",
    )(a2d, w, res2d)


def _rmsnorm_kernel(x_ref, g_ref, o_ref):
    o_ref[...] = _rms(x_ref[...], g_ref[...])


def _rmsnorm(x2d, g, *, tm=512):
    t, d = x2d.shape
    assert t % tm == 0
    return pl.pallas_call(
        _rmsnorm_kernel,
        grid=(t // tm,),
        in_specs=[pl.BlockSpec((tm, d), lambda i: (i, 0)), pl.BlockSpec((1, d), lambda i: (0, 0))],
        out_specs=pl.BlockSpec((tm, d), lambda i: (i, 0)),
        out_shape=jax.ShapeDtypeStruct((t, d), F32),
        compiler_params=_params("arbitrary"),
        name="rmsnorm",
    )(x2d, g.reshape(1, d))


def _s5_kernel(u_ref, wb_ref, tab_ref, pw_ref, wc_ref, dskip_ref, o_ref,
               up_scr, st_scr, xs_scr, y_scr, carry_scr):
    tc = u_ref.shape[0]
    ns = wb_ref.shape[1] // 2
    n_slabs = tc // SCAN_ROWS

    planes = up_scr.shape[0]
    for r in range(SCAN_ROWS):
        for c in range(planes):
            up_scr[c, pl.ds(r, n_slabs, stride=SCAN_ROWS), :] = (
                u_ref[r * n_slabs:(r + 1) * n_slabs, c * LANES:(c + 1) * LANES])
    u_perm = jnp.concatenate([up_scr[c] for c in range(planes)], axis=1)
    st_scr[...] = _dot(u_perm.astype(BF16), wb_ref[...])

    @pl.when(pl.program_id(2) == 0)
    def _():
        carry_scr[...] = jnp.zeros(carry_scr.shape, F32)

    ar = tab_ref[0]
    ai = tab_ref[1]

    def slab(s, x):
        xr, xi = x
        r0 = pl.multiple_of(s * SCAN_ROWS, SCAN_ROWS)
        xr, xi = (ar * xr - ai * xi + st_scr[pl.ds(r0, SCAN_ROWS), 0:ns],
                  ar * xi + ai * xr + st_scr[pl.ds(r0, SCAN_ROWS), ns:2 * ns])
        st_scr[pl.ds(r0, SCAN_ROWS), 0:ns] = xr
        st_scr[pl.ds(r0, SCAN_ROWS), ns:2 * ns] = xi
        return xr, xi

    zero = jnp.zeros((SCAN_ROWS, ns), F32)
    yr, yi = lax.fori_loop(0, n_slabs, slab, (zero, zero), unroll=2)

    cr = carry_scr[0]
    ci = carry_scr[1]
    for step, lag in enumerate((1, 2, 4)):
        mr = tab_ref[2 + 2 * step]
        mi = tab_ref[3 + 2 * step]
        sr = pltpu.roll(yr, lag, axis=0)
        si = pltpu.roll(yi, lag, axis=0)
        yr, yi = yr + mr * sr - mi * si, yi + mr * si + mi * sr
    pr = tab_ref[8]
    pi = tab_ref[9]
    yr, yi = yr + pr * cr - pi * ci, yi + pr * ci + pi * cr
    first = lax.broadcasted_iota(jnp.int32, (SCAN_ROWS, ns), 0) == 0
    init_r = jnp.where(first, cr, pltpu.roll(yr, 1, axis=0))
    init_i = jnp.where(first, ci, pltpu.roll(yi, 1, axis=0))
    last = SCAN_ROWS - 1
    carry_scr[0] = jnp.broadcast_to(yr[last:, :], (SCAN_ROWS, ns))
    carry_scr[1] = jnp.broadcast_to(yi[last:, :], (SCAN_ROWS, ns))

    def fix(s2, _):
        xr, xi = [], []
        for k in range(2):
            s = 2 * s2 + k
            r0 = pl.multiple_of(s * SCAN_ROWS, SCAN_ROWS)
            qr = pw_ref[0, pl.ds(s, 1), :]
            qi = pw_ref[1, pl.ds(s, 1), :]
            xr.append(st_scr[pl.ds(r0, SCAN_ROWS), 0:ns] + qr * init_r - qi * init_i)
            xi.append(st_scr[pl.ds(r0, SCAN_ROWS), ns:2 * ns] + qr * init_i + qi * init_r)
        r2 = pl.multiple_of(s2 * 2 * SCAN_ROWS, 2 * SCAN_ROWS)
        xs_scr[pl.ds(r2, 2 * SCAN_ROWS), 0:ns] = jnp.concatenate(xr, axis=0).astype(BF16)
        xs_scr[pl.ds(r2, 2 * SCAN_ROWS), ns:2 * ns] = jnp.concatenate(xi, axis=0).astype(BF16)
        return 0

    lax.fori_loop(0, n_slabs // 2, fix, 0)

    y_perm = _dot(xs_scr[...], wc_ref[...])
    for c in range(planes):
        y_scr[c] = y_perm[:, c * LANES:(c + 1) * LANES]
    for r in range(SCAN_ROWS):
        rows = slice(r * n_slabs, (r + 1) * n_slabs)
        for c in range(planes):
            cols = slice(c * LANES, (c + 1) * LANES)
            y = y_scr[c, pl.ds(r, n_slabs, stride=SCAN_ROWS), :] + dskip_ref[:, cols] * u_ref[rows, cols]
            o_ref[rows, cols] = jax.nn.gelu(y, approximate=True).astype(BF16)


def _s5_core(h2d, wb, tab, pw, wc, d_skip, *, bsz, seq):
    t, d = h2d.shape
    nblk, cb, ns2 = wb.shape
    n_slabs = pw.shape[2]
    tc = n_slabs * SCAN_ROWS
    assert seq % tc == 0 and n_slabs % 2 == 0 and nblk * cb == d
    nt = seq // tc
    return pl.pallas_call(
        _s5_kernel,
        grid=(bsz, nblk, nt),
        in_specs=[
            pl.BlockSpec((tc, cb), lambda b, gb, tt: (b * nt + tt, gb)),
            pl.BlockSpec((None, cb, ns2), lambda b, gb, tt: (gb, 0, 0)),
            pl.BlockSpec((None, 10, SCAN_ROWS, ns2 // 2), lambda b, gb, tt: (gb, 0, 0, 0)),
            pl.BlockSpec((None, 2, n_slabs, ns2 // 2), lambda b, gb, tt: (gb, 0, 0, 0)),
            pl.BlockSpec((None, ns2, cb), lambda b, gb, tt: (gb, 0, 0)),
            pl.BlockSpec((1, cb), lambda b, gb, tt: (0, gb)),
        ],
        out_specs=pl.BlockSpec((tc, cb), lambda b, gb, tt: (b * nt + tt, gb)),
        out_shape=jax.ShapeDtypeStruct((t, d), BF16),
        scratch_shapes=[
            pltpu.VMEM((cb // LANES, tc, LANES), F32),
            pltpu.VMEM((tc, ns2), F32),
            pltpu.VMEM((tc, ns2), BF16),
            pltpu.VMEM((cb // LANES, tc, LANES), F32),
            pltpu.VMEM((2, SCAN_ROWS, ns2 // 2), F32),
        ],
        compiler_params=_params("arbitrary", "arbitrary", "arbitrary"),
        name="s5_scan",
    )(h2d, wb, tab, pw, wc, d_skip.reshape(1, d))


def _glu_res_kernel(a_ref, wv_ref, wg_ref, bv_ref, bg_ref, r_ref, o_ref, *, sub):
    a = a_ref[...]
    tn = wv_ref.shape[1]
    pairs = []
    for c in range(tn // sub):
        cols = slice(c * sub, (c + 1) * sub)
        pairs.append((cols, _dot(a, wv_ref[:, cols]), _dot(a, wg_ref[:, cols])))
    for cols, val, gate in pairs:
        o_ref[:, cols] = r_ref[:, cols] + (val + bv_ref[:, cols]) * jax.nn.sigmoid(gate + bg_ref[:, cols])


def _glu_residual(a2d, w, b, res2d, *, tm=1024, tn=512, sub=256):
    t, k = a2d.shape
    n = w.shape[1] // 2
    assert t % tm == 0 and n % tn == 0 and tn % sub == 0
    nj = n // tn
    return pl.pallas_call(
        functools.partial(_glu_res_kernel, sub=sub),
        grid=(t // tm, nj),
        in_specs=[
            pl.BlockSpec((tm, k), lambda i, j: (i, 0)),
            pl.BlockSpec((k, tn), lambda i, j: (0, j)),
            pl.BlockSpec((k, tn), lambda i, j: (0, nj + j)),
            pl.BlockSpec((1, tn), lambda i, j: (0, j)),
            pl.BlockSpec((1, tn), lambda i, j: (0, nj + j)),
            pl.BlockSpec((tm, tn), lambda i, j: (i, j)),
        ],
        out_specs=pl.BlockSpec((tm, tn), lambda i, j: (i, j)),
        out_shape=jax.ShapeDtypeStruct((t, n), F32),
        compiler_params=_params("arbitrary", "arbitrary"),
        name="glu_residual",
    )(a2d, w, w, b.reshape(1, -1), b.reshape(1, -1), res2d)


def _s5_tables(lam_re, lam_im, log_step, b_re, b_im, c_re, c_im, *, n_slabs):
    groups, p = lam_re.shape
    hc = b_re.shape[2]
    gpb = SSM_GROUPS_PER_BLOCK
    nblk = groups // gpb
    step = jnp.exp(log_step)[:, None]
    mag = jnp.exp(lam_re * step)
    a_re = mag * jnp.cos(lam_im * step)
    a_im = mag * jnp.sin(lam_im * step)
    den = lam_re * lam_re + lam_im * lam_im
    f_re = ((a_re - 1.0) * lam_re + a_im * lam_im) / den
    f_im = (a_im * lam_re - (a_re - 1.0) * lam_im) / den
    bb_re = f_re[..., None] * b_re - f_im[..., None] * b_im
    bb_im = f_re[..., None] * b_im + f_im[..., None] * b_re
    eye = jnp.eye(gpb, dtype=F32)

    def block_diag_in(bb):
        bb = bb.reshape(nblk, gpb, p, hc).transpose(0, 1, 3, 2)
        return jnp.einsum('bghp,gk->bghkp', bb, eye).reshape(nblk, gpb * hc, gpb * p)

    def block_diag_out(c):
        c = c.reshape(nblk, gpb, hc, p).transpose(0, 1, 3, 2)
        return jnp.einsum('bgpo,gk->bgpko', c, eye).reshape(nblk, gpb * p, gpb * hc)

    wb = jnp.concatenate([block_diag_in(bb_re), block_diag_in(bb_im)], axis=2).astype(BF16)
    wc = jnp.concatenate([block_diag_out(c_re), -block_diag_out(c_im)], axis=1).astype(BF16)

    def powers(zr, zi, count):
        pr, pi = zr[:, None, :], zi[:, None, :]
        while pr.shape[1] < count:
            tr, ti = pr[:, -1:, :], pi[:, -1:, :]
            pr, pi = (jnp.concatenate([pr, pr * tr - pi * ti], axis=1),
                      jnp.concatenate([pi, pr * ti + pi * tr], axis=1))
        return pr[:, :count], pi[:, :count]

    ar = a_re.reshape(nblk, gpb * p)
    ai = a_im.reshape(nblk, gpb * p)
    pw_re, pw_im = powers(ar, ai, n_slabs)
    pw = jnp.stack([pw_re, pw_im], axis=1)
    lr, li = powers(pw_re[:, -1], pw_im[:, -1], SCAN_ROWS)
    rows = jnp.arange(SCAN_ROWS)[None, :, None]
    ones = jnp.ones((1, SCAN_ROWS, 1), F32)
    tabs = [ar[:, None, :] * ones, ai[:, None, :] * ones]
    for lag in (1, 2, 4):
        for z in (lr[:, lag - 1], li[:, lag - 1]):
            tabs.append(jnp.where(rows >= lag, z[:, None, :], 0.0))
    tabs += [lr, li]
    tab = jnp.stack(tabs, axis=1)
    return wb, tab, pw, wc


def kernel(x, norm_mix_g, norm_ffn_g, pool_w, pool_b, pool_scale, sb_w_qkv, sb_q_gain, sb_k_gain, sb_w_o, ssm_lam_re, ssm_lam_im, ssm_log_step, ssm_b_re, ssm_b_im, ssm_c_re, ssm_c_im, ssm_d, ssm_w_glu, ssm_b_glu, ffn_w_up, ffn_conv_w, ffn_conv_b, ffn_w_down):
    bsz, seq, d = x.shape
    depth = norm_mix_g.shape[0]
    heads = d // SB_HEAD_DIM
    x2d = x.reshape(bsz * seq, d)
    ffn_w_up = ffn_w_up.astype(BF16)
    ffn_w_down = ffn_w_down.astype(BF16)
    ffn_conv_b = ffn_conv_b.reshape(depth, 1, -1)
    for i in range(depth):
        kind = i % 3
        j = i // 3
        if kind == 0:
            x2d = _pool_mixer(x2d, norm_mix_g[i], pool_w[j].astype(BF16), pool_b[j], pool_scale[j],
                              seq=seq)
        elif kind == 1:
            gain_row = jnp.concatenate([
                jnp.tile(sb_q_gain[j] * (math.log2(math.e) / math.sqrt(SB_HEAD_DIM)), heads),
                jnp.tile(sb_k_gain[j], heads),
                jnp.ones((d,), F32)]).reshape(1, 3 * d)
            qkv = _qkv_proj(x2d, norm_mix_g[i], sb_w_qkv[j].astype(BF16), gain_row)
            o = _sb_attention(qkv.reshape(bsz, seq, 3 * d))
            x2d = _proj_residual(o.reshape(bsz * seq, d), sb_w_o[j].astype(BF16), x2d)
        else:
            wb, tab, pw, wc = _s5_tables(ssm_lam_re[j], ssm_lam_im[j], ssm_log_step[j],
                                         ssm_b_re[j], ssm_b_im[j], ssm_c_re[j], ssm_c_im[j],
                                         n_slabs=S5_TIME_CHUNK // SCAN_ROWS)
            h = _rmsnorm(x2d, norm_mix_g[i])
            y = _s5_core(h, wb, tab, pw, wc, ssm_d[j], bsz=bsz, seq=seq)
            x2d = _glu_residual(y, ssm_w_glu[j].astype(BF16), ssm_b_glu[j], x2d)
        x2d = _conv_ffn(x2d, norm_ffn_g[i], ffn_w_up, ffn_conv_w, ffn_conv_b, ffn_w_down,
                        layer=i, seq=seq)
    return x2d.reshape(bsz, seq, d)
```

```python
import functools
import math

import jax
import jax.numpy as jnp
from jax import lax
from jax.experimental import pallas as pl
from jax.experimental.pallas import tpu as pltpu

RMS_EPS = 1e-6
POOL_WINDOWS = (2, 4, 8, 16)
POOL_HALO = 32
SB_HEAD_DIM = 128
SSM_GROUP_CH = 16
SSM_STATE = 64
SSM_GROUPS_PER_BLOCK = 16
SCAN_ROWS = 8
S5_TIME_CHUNK = 1024
LANES = 128
CONV_WIDTH = 3

V7X_VMEM_LIMIT_BYTES = 60 * 1024 * 1024

F32 = jnp.float32
BF16 = jnp.bfloat16


def _params(*semantics):
    return pltpu.CompilerParams(dimension_semantics=semantics,
                                vmem_limit_bytes=V7X_VMEM_LIMIT_BYTES)


def _rms(x, g):
    ms = jnp.mean(x * x, axis=-1, keepdims=True)
    return x * lax.rsqrt(ms + RMS_EPS) * g


def _dot(a, b):
    return jnp.dot(a, b, preferred_element_type=F32)


def _ffn_kernel(x_ref, g_ref, wv_ref, wg_ref, cwv_ref, cwg_ref, cbv_ref, cbg_ref, wd_ref,
                o_ref, h_scr, act_a, act_b, tail_v, tail_g, *, tiles_per_seq):
    i = pl.program_id(0)
    j = pl.program_id(1)
    n_tiles = pl.num_programs(0) - 1
    nf = pl.num_programs(1)
    tm = x_ref.shape[0]
    fc = wv_ref.shape[1]
    step = i * nf + j

    @pl.when(step == 0)
    def _():
        act_b[...] = jnp.zeros(act_b.shape, BF16)
        o_ref[...] = jnp.zeros(o_ref.shape, F32)

    @pl.when((j == 0) & (i < n_tiles))
    def _():
        h_scr[...] = _rms(x_ref[...], g_ref[...]).astype(BF16)

        @pl.when(i % tiles_per_seq == 0)
        def _():
            tail_v[...] = jnp.zeros(tail_v.shape, F32)
            tail_g[...] = jnp.zeros(tail_g.shape, F32)

    @pl.when((j == 1) & (i < n_tiles))
    def _():
        o_ref[...] = x_ref[...]

    chunk = jnp.where(i < n_tiles, j, nf - 1)
    active = (i < n_tiles) | (j == 0)
    row = lax.broadcasted_iota(jnp.int32, (SCAN_ROWS, fc), 0)

    def causal_conv(up, tail_ref, cw_ref, cb_ref):
        prev = tail_ref[chunk]
        tail_ref[chunk] = up[tm - SCAN_ROWS:, :]
        out = cb_ref[...] + cw_ref[2:3, :] * up
        for lag in (1, 2):
            shifted = pltpu.roll(up, lag, axis=0)
            head = jnp.where(row < lag, pltpu.roll(prev, lag, axis=0), shifted[:SCAN_ROWS])
            shifted = jnp.concatenate([head, shifted[SCAN_ROWS:]], axis=0)
            out = out + cw_ref[2 - lag:3 - lag, :] * shifted
        return out

    def stages(act_new, act_old):
        h = h_scr[...]
        val = causal_conv(_dot(h, wv_ref[...]), tail_v, cwv_ref, cbv_ref)
        up_gate = _dot(h, wg_ref[...])
        o_ref[...] += _dot(act_old[...], wd_ref[...])
        gate = causal_conv(up_gate, tail_g, cwg_ref, cbg_ref)
        half = 0.5 * gate
        act_new[...] = ((half + half * jnp.tanh(half)) * val).astype(BF16)

    pl.when(active & (step % 2 == 0))(lambda: stages(act_a, act_b))
    pl.when(active & (step % 2 == 1))(lambda: stages(act_b, act_a))


def _conv_ffn(x2d, g, w_up, conv_w, conv_b, w_down, *, layer, seq, tm=1024, fc=512):
    t, d = x2d.shape
    f = w_down.shape[1]
    nf = f // fc
    n_tiles = t // tm
    assert t % tm == 0 and seq % tm == 0 and f % fc == 0
    kernel = functools.partial(_ffn_kernel, tiles_per_seq=seq // tm)

    def up_chunk(i, j):
        return jnp.where(i < n_tiles, j, nf - 1)

    def down_chunk(i, j):
        return jnp.where((i < n_tiles) & (j > 0), j - 1, nf - 1)

    def down_tile(i, j):
        return jnp.clip(jnp.where(j == 0, i - 1, i), 0, n_tiles - 1)

    return pl.pallas_call(
        kernel,
        grid=(n_tiles + 1, nf),
        in_specs=[
            pl.BlockSpec((tm, d), lambda i, j: (jnp.minimum(i, n_tiles - 1), 0)),
            pl.BlockSpec((1, d), lambda i, j: (0, 0)),
            pl.BlockSpec((None, d, fc), lambda i, j: (layer, 0, up_chunk(i, j))),
            pl.BlockSpec((None, d, fc), lambda i, j: (layer, 0, nf + up_chunk(i, j))),
            pl.BlockSpec((None, CONV_WIDTH, fc), lambda i, j: (layer, 0, up_chunk(i, j))),
            pl.BlockSpec((None, CONV_WIDTH, fc), lambda i, j: (layer, 0, nf + up_chunk(i, j))),
            pl.BlockSpec((None, 1, fc), lambda i, j: (layer, 0, up_chunk(i, j))),
            pl.BlockSpec((None, 1, fc), lambda i, j: (layer, 0, nf + up_chunk(i, j))),
            pl.BlockSpec((None, fc, d), lambda i, j: (layer, down_chunk(i, j), 0)),
        ],
        out_specs=pl.BlockSpec((tm, d), lambda i, j: (down_tile(i, j), 0)),
        out_shape=jax.ShapeDtypeStruct((t, d), F32),
        scratch_shapes=[
            pltpu.VMEM((tm, d), BF16),
            pltpu.VMEM((tm, fc), BF16),
            pltpu.VMEM((tm, fc), BF16),
            pltpu.VMEM((nf, SCAN_ROWS, fc), F32),
            pltpu.VMEM((nf, SCAN_ROWS, fc), F32),
        ],
        compiler_params=_params("arbitrary", "arbitrary"),
        name="conv_ffn",
    )(x2d, g.reshape(1, d), w_up, w_up, conv_w, conv_w, conv_b, conv_b, w_down)


def _pool_kernel(x_ref, halo_ref, g_ref, w_ref, b_ref, s_ref, o_ref, hbuf, sum_a, sum_b,
                 *, tiles_per_seq):
    i = pl.program_id(0)
    tm, d = x_ref.shape
    dg = d // len(POOL_WINDOWS)
    x = x_ref[...]
    g = g_ref[...]
    h = _rms(x, g)
    tile_in_seq = i % tiles_per_seq
    halo = jnp.where(tile_in_seq == 0, 0.0, _rms(halo_ref[...], g))
    hbuf[0:POOL_HALO, :] = halo
    hbuf[POOL_HALO:, :] = h
    t_in_seq = tile_in_seq * tm + lax.broadcasted_iota(jnp.int32, (tm, dg), 0)
    for gi, win in enumerate(POOL_WINDOWS):
        cols = slice(gi * dg, (gi + 1) * dg)
        levels = win.bit_length() - 1
        src, src_cols = hbuf, cols
        for k in range(1, levels + 1):
            lag = 1 << (k - 1)
            start = POOL_HALO - SCAN_ROWS * (levels - k)
            rows = POOL_HALO + tm - start
            acc = src[start:start + rows, src_cols] + src[start - lag:start - lag + rows, src_cols]
            if k < levels:
                src, src_cols = (sum_a, sum_b)[k % 2], slice(0, dg)
                src[start:start + rows, :] = acc
        cnt = jnp.minimum(t_in_seq + 1, win).astype(F32)
        pooled = acc / cnt - h[:, cols]
        y = _dot(pooled.astype(BF16), w_ref[gi])
        o_ref[:, cols] = x[:, cols] + (y + b_ref[:, cols]) * s_ref[:, cols]


def _pool_mixer(x2d, g, w, b, scale, *, seq, tm=512):
    t, d = x2d.shape
    ng, dg, _ = w.shape
    assert t % tm == 0 and seq % tm == 0 and tm % POOL_HALO == 0
    halo_blocks_per_tile = tm // POOL_HALO
    kernel = functools.partial(_pool_kernel, tiles_per_seq=seq // tm)
    return pl.pallas_call(
        kernel,
        grid=(t // tm,),
        in_specs=[
            pl.BlockSpec((tm, d), lambda i: (i, 0)),
            pl.BlockSpec((POOL_HALO, d), lambda i: (jnp.maximum(i * halo_blocks_per_tile - 1, 0), 0)),
            pl.BlockSpec((1, d), lambda i: (0, 0)),
            pl.BlockSpec((ng, dg, dg), lambda i: (0, 0, 0)),
            pl.BlockSpec((1, d), lambda i: (0, 0)),
            pl.BlockSpec((1, d), lambda i: (0, 0)),
        ],
        out_specs=pl.BlockSpec((tm, d), lambda i: (i, 0)),
        out_shape=jax.ShapeDtypeStruct((t, d), F32),
        scratch_shapes=[pltpu.VMEM((POOL_HALO + tm, d), F32),
                        pltpu.VMEM((POOL_HALO + tm, dg), F32),
                        pltpu.VMEM((POOL_HALO + tm, dg), F32)],
        compiler_params=_params("arbitrary"),
        name="pool_mixer",
    )(x2d, x2d, g.reshape(1, d), w, b.reshape(1, d), scale.reshape(1, d))


def _qkv_kernel(x_ref, g_ref, w_ref, gain_ref, o_ref, h_scr, *, norm_tiles, sub):
    j = pl.program_id(1)
    tn = w_ref.shape[1]

    @pl.when(j == 0)
    def _():
        h_scr[...] = _rms(x_ref[...], g_ref[...]).astype(BF16)

    h = h_scr[...]
    normed = j < norm_tiles
    accs = [_dot(h, w_ref[:, c * sub:(c + 1) * sub]) for c in range(tn // sub)]
    for c, acc in enumerate(accs):
        for hh in range(sub // SB_HEAD_DIM):
            head = acc[:, hh * SB_HEAD_DIM:(hh + 1) * SB_HEAD_DIM]
            cols = slice(c * sub + hh * SB_HEAD_DIM, c * sub + (hh + 1) * SB_HEAD_DIM)
            o_ref[:, cols] = jnp.where(normed, _rms(head, gain_ref[:, cols]), head).astype(BF16)


def _qkv_proj(x2d, g, w_qkv, gain_row, *, tm=1024, tn=1024, sub=512):
    t, d = x2d.shape
    n = w_qkv.shape[1]
    assert t % tm == 0 and n % tn == 0 and (2 * d) % tn == 0 and sub % SB_HEAD_DIM == 0
    assert tn % sub == 0
    kernel = functools.partial(_qkv_kernel, norm_tiles=2 * d // tn, sub=sub)
    return pl.pallas_call(
        kernel,
        grid=(t // tm, n // tn),
        in_specs=[
            pl.BlockSpec((tm, d), lambda i, j: (i, 0)),
            pl.BlockSpec((1, d), lambda i, j: (0, 0)),
            pl.BlockSpec((d, tn), lambda i, j: (0, j)),
            pl.BlockSpec((1, tn), lambda i, j: (0, j)),
        ],
        out_specs=pl.BlockSpec((tm, tn), lambda i, j: (i, j)),
        out_shape=jax.ShapeDtypeStruct((t, n), BF16),
        scratch_shapes=[pltpu.VMEM((tm, d), BF16)],
        compiler_params=_params("arbitrary", "arbitrary"),
        name="sb_qkv",
    )(x2d, g.reshape(1, d), w_qkv, gain_row)


def _sb_attn_kernel(q_ref, k_ref, v_ref, o_ref, *, chain_rows):
    qi = pl.program_id(2)
    blk = q_ref.shape[0]
    dh = SB_HEAD_DIM
    n_heads = q_ref.shape[1] // dh
    row = lax.broadcasted_iota(jnp.int32, (blk, blk), 0)
    col = lax.broadcasted_iota(jnp.int32, (blk, blk), 1)
    later = jnp.where(row > col, 1.0, 0.0).astype(BF16)
    later2 = jnp.concatenate([later, later], axis=0)
    row_g = lax.broadcasted_iota(jnp.int32, (chain_rows, blk), 0)
    col_g = lax.broadcasted_iota(jnp.int32, (chain_rows, blk), 1)

    chains = [(hh, r0) for hh in range(n_heads) for r0 in range(0, blk, chain_rows)]

    def key_block(kb, state, diagonal):
        start = pl.multiple_of(kb * blk, blk)
        cols = [slice(hh * dh, (hh + 1) * dh) for hh, _ in chains]
        rows = [slice(r0, r0 + chain_rows) for _, r0 in chains]
        zs = [lax.dot_general(q_ref[r, c], k_ref[pl.ds(start, blk), c], (((1,), (1,)), ((), ())),
                              preferred_element_type=F32) for r, c in zip(rows, cols)]
        log_betas, log_1m_betas, splits = [], [], []
        for (_, r0), z in zip(chains, zs):
            log_beta = jnp.minimum(z, 0.0) - jnp.log2(1.0 + jnp.exp2(-jnp.abs(z)))
            log_1m_beta = log_beta - z
            if diagonal:
                log_1m_beta = jnp.where(col_g < row_g + r0, log_1m_beta, 0.0)
            hi = log_1m_beta.astype(BF16)
            lo = (log_1m_beta - hi.astype(F32)).astype(BF16)
            log_betas.append(log_beta)
            log_1m_betas.append(log_1m_beta)
            splits.append(jnp.concatenate([hi, lo], axis=1))
        remains = [_dot(s, later2) for s in splits]
        ws = []
        for (_, r0), log_beta, remain, (carry, _) in zip(chains, log_betas, remains, state):
            w = jnp.exp2(log_beta + remain + carry)
            if diagonal:
                w = jnp.where(col_g < row_g + r0, w, 0.0)
            ws.append(w.astype(BF16))
        out = []
        for c, w, log_1m_beta, (carry, acc) in zip(cols, ws, log_1m_betas, state):
            out.append((carry + jnp.sum(log_1m_beta, axis=-1, keepdims=True),
                        acc + _dot(w, v_ref[pl.ds(start, blk), c])))
        return tuple(out)

    zero = (jnp.zeros((chain_rows, 1), F32), jnp.zeros((chain_rows, dh), F32))
    state = key_block(qi, (zero,) * len(chains), True)
    state = lax.fori_loop(0, qi, lambda n, st: key_block(qi - 1 - n, st, False), state)
    for (hh, r0), (_, acc) in zip(chains, state):
        o_ref[r0:r0 + chain_rows, hh * dh:(hh + 1) * dh] = acc.astype(BF16)


def _sb_attention(qkv3d, *, blk=256, heads_per_step=4, chain_rows=256):
    bsz, seq, n3 = qkv3d.shape
    d = n3 // 3
    wide = heads_per_step * SB_HEAD_DIM
    groups = d // wide
    assert seq % blk == 0 and d % wide == 0 and blk % chain_rows == 0
    return pl.pallas_call(
        functools.partial(_sb_attn_kernel, chain_rows=chain_rows),
        grid=(bsz, groups, seq // blk),
        in_specs=[
            pl.BlockSpec((None, blk, wide), lambda b, h, qi: (b, qi, h)),
            pl.BlockSpec((None, seq, wide), lambda b, h, qi: (b, 0, groups + h)),
            pl.BlockSpec((None, seq, wide), lambda b, h, qi: (b, 0, 2 * groups + h)),
        ],
        out_specs=pl.BlockSpec((None, blk, wide), lambda b, h, qi: (b, qi, h)),
        out_shape=jax.ShapeDtypeStruct((bsz, seq, d), BF16),
        compiler_params=_params("arbitrary", "arbitrary", "arbitrary"),
        name="sb_attention",
    )(qkv3d, qkv3d, qkv3d)


def _proj_res_kernel(a_ref, w_ref, r_ref, o_ref, *, sub):
    a = a_ref[...]
    tn = w_ref.shape[1]
    accs = [_dot(a, w_ref[:, c * sub:(c + 1) * sub]) for c in range(tn // sub)]
    for c, acc in enumerate(accs):
        cols = slice(c * sub, (c + 1) * sub)
        o_ref[:, cols] = r_ref[:, cols] + acc


def _proj_residual(a2d, w, res2d, *, tm=1024, tn=1024, sub=512):
    t, k = a2d.shape
    n = w.shape[1]
    assert t % tm == 0 and n % tn == 0 and tn % sub == 0
    return pl.pallas_call(
        functools.partial(_proj_res_kernel, sub=sub),
        grid=(t // tm, n // tn),
        in_specs=[
            pl.BlockSpec((tm, k), lambda i, j: (i, 0)),
            pl.BlockSpec((k, tn), lambda i, j: (0, j)),
            pl.BlockSpec((tm, tn), lambda i, j: (i, j)),
        ],
        out_specs=pl.BlockSpec((tm, tn), lambda i, j: (i, j)),
        out_shape=jax.ShapeDtypeStruct((t, n), F32),
        compiler_params=_params("arbitrary", "arbitrary"),
        name="proj_residual",
    )(a2d, w, res2d)


def _rmsnorm_kernel(x_ref, g_ref, o_ref):
    o_ref[...] = _rms(x_ref[...], g_ref[...])


def _rmsnorm(x2d, g, *, tm=512):
    t, d = x2d.shape
    assert t % tm == 0
    return pl.pallas_call(
        _rmsnorm_kernel,
        grid=(t // tm,),
        in_specs=[pl.BlockSpec((tm, d), lambda i: (i, 0)), pl.BlockSpec((1, d), lambda i: (0, 0))],
        out_specs=pl.BlockSpec((tm, d), lambda i: (i, 0)),
        out_shape=jax.ShapeDtypeStruct((t, d), F32),
        compiler_params=_params("arbitrary"),
        name="rmsnorm",
    )(x2d, g.reshape(1, d))


def _s5_kernel(u_ref, wb_ref, tab_ref, pw_ref, wc_ref, dskip_ref, o_ref,
               up_scr, st_scr, xs_scr, y_scr, carry_scr):
    tc = u_ref.shape[0]
    ns = wb_ref.shape[1] // 2
    n_slabs = tc // SCAN_ROWS

    planes = up_scr.shape[0]
    for r in range(SCAN_ROWS):
        for c in range(planes):
            up_scr[c, pl.ds(r, n_slabs, stride=SCAN_ROWS), :] = (
                u_ref[r * n_slabs:(r + 1) * n_slabs, c * LANES:(c + 1) * LANES])
    u_perm = jnp.concatenate([up_scr[c] for c in range(planes)], axis=1)
    st_scr[...] = _dot(u_perm.astype(BF16), wb_ref[...])

    @pl.when(pl.program_id(2) == 0)
    def _():
        carry_scr[...] = jnp.zeros(carry_scr.shape, F32)

    ar = tab_ref[0]
    ai = tab_ref[1]

    def slab(s, x):
        xr, xi = x
        r0 = pl.multiple_of(s * SCAN_ROWS, SCAN_ROWS)
        xr, xi = (ar * xr - ai * xi + st_scr[pl.ds(r0, SCAN_ROWS), 0:ns],
                  ar * xi + ai * xr + st_scr[pl.ds(r0, SCAN_ROWS), ns:2 * ns])
        st_scr[pl.ds(r0, SCAN_ROWS), 0:ns] = xr
        st_scr[pl.ds(r0, SCAN_ROWS), ns:2 * ns] = xi
        return xr, xi

    zero = jnp.zeros((SCAN_ROWS, ns), F32)
    yr, yi = lax.fori_loop(0, n_slabs, slab, (zero, zero), unroll=2)

    cr = carry_scr[0]
    ci = carry_scr[1]
    for step, lag in enumerate((1, 2, 4)):
        mr = tab_ref[2 + 2 * step]
        mi = tab_ref[3 + 2 * step]
        sr = pltpu.roll(yr, lag, axis=0)
        si = pltpu.roll(yi, lag, axis=0)
        yr, yi = yr + mr * sr - mi * si, yi + mr * si + mi * sr
    pr = tab_ref[8]
    pi = tab_ref[9]
    yr, yi = yr + pr * cr - pi * ci, yi + pr * ci + pi * cr
    first = lax.broadcasted_iota(jnp.int32, (SCAN_ROWS, ns), 0) == 0
    init_r = jnp.where(first, cr, pltpu.roll(yr, 1, axis=0))
    init_i = jnp.where(first, ci, pltpu.roll(yi, 1, axis=0))
    last = SCAN_ROWS - 1
    carry_scr[0] = jnp.broadcast_to(yr[last:, :], (SCAN_ROWS, ns))
    carry_scr[1] = jnp.broadcast_to(yi[last:, :], (SCAN_ROWS, ns))

    def fix(s2, _):
        xr, xi = [], []
        for k in range(2):
            s = 2 * s2 + k
            r0 = pl.multiple_of(s * SCAN_ROWS, SCAN_ROWS)
            qr = pw_ref[0, pl.ds(s, 1), :]
            qi = pw_ref[1, pl.ds(s, 1), :]
            xr.append(st_scr[pl.ds(r0, SCAN_ROWS), 0:ns] + qr * init_r - qi * init_i)
            xi.append(st_scr[pl.ds(r0, SCAN_ROWS), ns:2 * ns] + qr * init_i + qi * init_r)
        r2 = pl.multiple_of(s2 * 2 * SCAN_ROWS, 2 * SCAN_ROWS)
        xs_scr[pl.ds(r2, 2 * SCAN_ROWS), 0:ns] = jnp.concatenate(xr, axis=0).astype(BF16)
        xs_scr[pl.ds(r2, 2 * SCAN_ROWS), ns:2 * ns] = jnp.concatenate(xi, axis=0).astype(BF16)
        return 0

    lax.fori_loop(0, n_slabs // 2, fix, 0)

    half_rows = tc // 2
    y_halves = [_dot(xs_scr[k * half_rows:(k + 1) * half_rows, :], wc_ref[...]) for k in range(2)]
    for k, y_half in enumerate(y_halves):
        for c in range(planes):
            y_scr[c, k * half_rows:(k + 1) * half_rows, :] = y_half[:, c * LANES:(c + 1) * LANES]
    for r in range(SCAN_ROWS):
        rows = slice(r * n_slabs, (r + 1) * n_slabs)
        for c in range(planes):
            cols = slice(c * LANES, (c + 1) * LANES)
            y = y_scr[c, pl.ds(r, n_slabs, stride=SCAN_ROWS), :] + dskip_ref[:, cols] * u_ref[rows, cols]
            o_ref[rows, cols] = jax.nn.gelu(y, approximate=True).astype(BF16)


def _s5_core(h2d, wb, tab, pw, wc, d_skip, *, bsz, seq):
    t, d = h2d.shape
    nblk, cb, ns2 = wb.shape
    n_slabs = pw.shape[2]
    tc = n_slabs * SCAN_ROWS
    assert seq % tc == 0 and n_slabs % 2 == 0 and nblk * cb == d
    nt = seq // tc
    return pl.pallas_call(
        _s5_kernel,
        grid=(bsz, nblk, nt),
        in_specs=[
            pl.BlockSpec((tc, cb), lambda b, gb, tt: (b * nt + tt, gb)),
            pl.BlockSpec((None, cb, ns2), lambda b, gb, tt: (gb, 0, 0)),
            pl.BlockSpec((None, 10, SCAN_ROWS, ns2 // 2), lambda b, gb, tt: (gb, 0, 0, 0)),
            pl.BlockSpec((None, 2, n_slabs, ns2 // 2), lambda b, gb, tt: (gb, 0, 0, 0)),
            pl.BlockSpec((None, ns2, cb), lambda b, gb, tt: (gb, 0, 0)),
            pl.BlockSpec((1, cb), lambda b, gb, tt: (0, gb)),
        ],
        out_specs=pl.BlockSpec((tc, cb), lambda b, gb, tt: (b * nt + tt, gb)),
        out_shape=jax.ShapeDtypeStruct((t, d), BF16),
        scratch_shapes=[
            pltpu.VMEM((cb // LANES, tc, LANES), F32),
            pltpu.VMEM((tc, ns2), F32),
            pltpu.VMEM((tc, ns2), BF16),
            pltpu.VMEM((cb // LANES, tc, LANES), F32),
            pltpu.VMEM((2, SCAN_ROWS, ns2 // 2), F32),
        ],
        compiler_params=_params("arbitrary", "arbitrary", "arbitrary"),
        name="s5_scan",
    )(h2d, wb, tab, pw, wc, d_skip.reshape(1, d))


def _glu_res_kernel(a_ref, wv_ref, wg_ref, bv_ref, bg_ref, r_ref, o_ref, *, sub):
    a = a_ref[...]
    tn = wv_ref.shape[1]
    pairs = []
    for c in range(tn // sub):
        cols = slice(c * sub, (c + 1) * sub)
        pairs.append((cols, _dot(a, wv_ref[:, cols]), _dot(a, wg_ref[:, cols])))
    for cols, val, gate in pairs:
        o_ref[:, cols] = r_ref[:, cols] + (val + bv_ref[:, cols]) * jax.nn.sigmoid(gate + bg_ref[:, cols])


def _glu_residual(a2d, w, b, res2d, *, tm=1024, tn=512, sub=256):
    t, k = a2d.shape
    n = w.shape[1] // 2
    assert t % tm == 0 and n % tn == 0 and tn % sub == 0
    nj = n // tn
    return pl.pallas_call(
        functools.partial(_glu_res_kernel, sub=sub),
        grid=(t // tm, nj),
        in_specs=[
            pl.BlockSpec((tm, k), lambda i, j: (i, 0)),
            pl.BlockSpec((k, tn), lambda i, j: (0, j)),
            pl.BlockSpec((k, tn), lambda i, j: (0, nj + j)),
            pl.BlockSpec((1, tn), lambda i, j: (0, j)),
            pl.BlockSpec((1, tn), lambda i, j: (0, nj + j)),
            pl.BlockSpec((tm, tn), lambda i, j: (i, j)),
        ],
        out_specs=pl.BlockSpec((tm, tn), lambda i, j: (i, j)),
        out_shape=jax.ShapeDtypeStruct((t, n), F32),
        compiler_params=_params("arbitrary", "arbitrary"),
        name="glu_residual",
    )(a2d, w, w, b.reshape(1, -1), b.reshape(1, -1), res2d)


def _s5_tables(lam_re, lam_im, log_step, b_re, b_im, c_re, c_im, *, n_slabs):
    groups, p = lam_re.shape
    hc = b_re.shape[2]
    gpb = SSM_GROUPS_PER_BLOCK
    nblk = groups // gpb
    step = jnp.exp(log_step)[:, None]
    mag = jnp.exp(lam_re * step)
    a_re = mag * jnp.cos(lam_im * step)
    a_im = mag * jnp.sin(lam_im * step)
    den = lam_re * lam_re + lam_im * lam_im
    f_re = ((a_re - 1.0) * lam_re + a_im * lam_im) / den
    f_im = (a_im * lam_re - (a_re - 1.0) * lam_im) / den
    bb_re = f_re[..., None] * b_re - f_im[..., None] * b_im
    bb_im = f_re[..., None] * b_im + f_im[..., None] * b_re
    eye = jnp.eye(gpb, dtype=F32)

    def block_diag_in(bb):
        bb = bb.reshape(nblk, gpb, p, hc).transpose(0, 1, 3, 2)
        return jnp.einsum('bghp,gk->bghkp', bb, eye).reshape(nblk, gpb * hc, gpb * p)

    def block_diag_out(c):
        c = c.reshape(nblk, gpb, hc, p).transpose(0, 1, 3, 2)
        return jnp.einsum('bgpo,gk->bgpko', c, eye).reshape(nblk, gpb * p, gpb * hc)

    wb = jnp.concatenate([block_diag_in(bb_re), block_diag_in(bb_im)], axis=2).astype(BF16)
    wc = jnp.concatenate([block_diag_out(c_re), -block_diag_out(c_im)], axis=1).astype(BF16)

    def powers(zr, zi, count):
        pr, pi = zr[:, None, :], zi[:, None, :]
        while pr.shape[1] < count:
            tr, ti = pr[:, -1:, :], pi[:, -1:, :]
            pr, pi = (jnp.concatenate([pr, pr * tr - pi * ti], axis=1),
                      jnp.concatenate([pi, pr * ti + pi * tr], axis=1))
        return pr[:, :count], pi[:, :count]

    ar = a_re.reshape(nblk, gpb * p)
    ai = a_im.reshape(nblk, gpb * p)
    pw_re, pw_im = powers(ar, ai, n_slabs)
    pw = jnp.stack([pw_re, pw_im], axis=1)
    lr, li = powers(pw_re[:, -1], pw_im[:, -1], SCAN_ROWS)
    rows = jnp.arange(SCAN_ROWS)[None, :, None]
    ones = jnp.ones((1, SCAN_ROWS, 1), F32)
    tabs = [ar[:, None, :] * ones, ai[:, None, :] * ones]
    for lag in (1, 2, 4):
        for z in (lr[:, lag - 1], li[:, lag - 1]):
            tabs.append(jnp.where(rows >= lag, z[:, None, :], 0.0))
    tabs += [lr, li]
    tab = jnp.stack(tabs, axis=1)
    return wb, tab, pw, wc


def kernel(x, norm_mix_g, norm_ffn_g, pool_w, pool_b, pool_scale, sb_w_qkv, sb_q_gain, sb_k_gain, sb_w_o, ssm_lam_re, ssm_lam_im, ssm_log_step, ssm_b_re, ssm_b_im, ssm_c_re, ssm_c_im, ssm_d, ssm_w_glu, ssm_b_glu, ffn_w_up, ffn_conv_w, ffn_conv_b, ffn_w_down):
    bsz, seq, d = x.shape
    depth = norm_mix_g.shape[0]
    heads = d // SB_HEAD_DIM
    x2d = x.reshape(bsz * seq, d)
    ffn_w_up = ffn_w_up.astype(BF16)
    ffn_w_down = ffn_w_down.astype(BF16)
    ffn_conv_b = ffn_conv_b.reshape(depth, 1, -1)
    for i in range(depth):
        kind = i % 3
        j = i // 3
        if kind == 0:
            x2d = _pool_mixer(x2d, norm_mix_g[i], pool_w[j].astype(BF16), pool_b[j], pool_scale[j],
                              seq=seq)
        elif kind == 1:
            gain_row = jnp.concatenate([
                jnp.tile(sb_q_gain[j] * (math.log2(math.e) / math.sqrt(SB_HEAD_DIM)), heads),
                jnp.tile(sb_k_gain[j], heads),
                jnp.ones((d,), F32)]).reshape(1, 3 * d)
            qkv = _qkv_proj(x2d, norm_mix_g[i], sb_w_qkv[j].astype(BF16), gain_row)
            o = _sb_attention(qkv.reshape(bsz, seq, 3 * d))
            x2d = _proj_residual(o.reshape(bsz * seq, d), sb_w_o[j].astype(BF16), x2d)
        else:
            wb, tab, pw, wc = _s5_tables(ssm_lam_re[j], ssm_lam_im[j], ssm_log_step[j],
                                         ssm_b_re[j], ssm_b_im[j], ssm_c_re[j], ssm_c_im[j],
                                         n_slabs=S5_TIME_CHUNK // SCAN_ROWS)
            h = _rmsnorm(x2d, norm_mix_g[i])
            y = _s5_core(h, wb, tab, pw, wc, ssm_d[j], bsz=bsz, seq=seq)
            x2d = _glu_residual(y, ssm_w_glu[j].astype(BF16), ssm_b_glu[j], x2d)
        x2d = _conv_ffn(x2d, norm_ffn_g[i], ffn_w_up, ffn_conv_w, ffn_conv_b, ffn_w_down,
                        layer=i, seq=seq)
    return x2d.reshape(bsz, seq, d)
```

```python
import functools
import math

import jax
import jax.numpy as jnp
from jax import lax
from jax.experimental import pallas as pl
from jax.experimental.pallas import tpu as pltpu

RMS_EPS = 1e-6
POOL_WINDOWS = (2, 4, 8, 16)
POOL_HALO = 32
SB_HEAD_DIM = 128
SSM_GROUP_CH = 16
SSM_STATE = 64
SSM_GROUPS_PER_BLOCK = 16
SCAN_ROWS = 8
S5_TIME_CHUNK = 1024
LANES = 128
CONV_WIDTH = 3

V7X_VMEM_LIMIT_BYTES = 60 * 1024 * 1024

F32 = jnp.float32
BF16 = jnp.bfloat16


def _params(*semantics):
    return pltpu.CompilerParams(dimension_semantics=semantics,
                                vmem_limit_bytes=V7X_VMEM_LIMIT_BYTES)


def _rms(x, g):
    ms = jnp.mean(x * x, axis=-1, keepdims=True)
    return x * lax.rsqrt(ms + RMS_EPS) * g


def _dot(a, b):
    return jnp.dot(a, b, preferred_element_type=F32)


def _ffn_kernel(x_ref, g_ref, wv_ref, wg_ref, cwv_ref, cwg_ref, cbv_ref, cbg_ref, wd_ref,
                o_ref, h_scr, act_a, act_b, tail_v, tail_g, *, tiles_per_seq):
    i = pl.program_id(0)
    j = pl.program_id(1)
    n_tiles = pl.num_programs(0) - 1
    nf = pl.num_programs(1)
    tm = x_ref.shape[0]
    fc = wv_ref.shape[1]
    step = i * nf + j

    @pl.when(step == 0)
    def _():
        act_b[...] = jnp.zeros(act_b.shape, BF16)
        o_ref[...] = jnp.zeros(o_ref.shape, F32)

    @pl.when((j == 0) & (i < n_tiles))
    def _():
        h_scr[...] = _rms(x_ref[...], g_ref[...]).astype(BF16)

        @pl.when(i % tiles_per_seq == 0)
        def _():
            tail_v[...] = jnp.zeros(tail_v.shape, F32)
            tail_g[...] = jnp.zeros(tail_g.shape, F32)

    @pl.when((j == 1) & (i < n_tiles))
    def _():
        o_ref[...] = x_ref[...]

    chunk = jnp.where(i < n_tiles, j, nf - 1)
    active = (i < n_tiles) | (j == 0)
    row = lax.broadcasted_iota(jnp.int32, (SCAN_ROWS, fc), 0)

    def causal_conv(up, tail_ref, cw_ref, cb_ref):
        prev = tail_ref[chunk]
        tail_ref[chunk] = up[tm - SCAN_ROWS:, :]
        out = cb_ref[...] + cw_ref[2:3, :] * up
        for lag in (1, 2):
            shifted = pltpu.roll(up, lag, axis=0)
            head = jnp.where(row < lag, pltpu.roll(prev, lag, axis=0), shifted[:SCAN_ROWS])
            shifted = jnp.concatenate([head, shifted[SCAN_ROWS:]], axis=0)
            out = out + cw_ref[2 - lag:3 - lag, :] * shifted
        return out

    def stages(act_new, act_old):
        h = h_scr[...]
        val = causal_conv(_dot(h, wv_ref[...]), tail_v, cwv_ref, cbv_ref)
        up_gate = _dot(h, wg_ref[...])
        o_ref[...] += _dot(act_old[...], wd_ref[...])
        gate = causal_conv(up_gate, tail_g, cwg_ref, cbg_ref)
        half = 0.5 * gate
        act_new[...] = ((half + half * jnp.tanh(half)) * val).astype(BF16)

    pl.when(active & (step % 2 == 0))(lambda: stages(act_a, act_b))
    pl.when(active & (step % 2 == 1))(lambda: stages(act_b, act_a))


def _conv_ffn(x2d, g, w_up, conv_w, conv_b, w_down, *, layer, seq, tm=1024, fc=512):
    t, d = x2d.shape
    f = w_down.shape[1]
    nf = f // fc
    n_tiles = t // tm
    assert t % tm == 0 and seq % tm == 0 and f % fc == 0
    kernel = functools.partial(_ffn_kernel, tiles_per_seq=seq // tm)

    def up_chunk(i, j):
        return jnp.where(i < n_tiles, j, nf - 1)

    def down_chunk(i, j):
        return jnp.where((i < n_tiles) & (j > 0), j - 1, nf - 1)

    def down_tile(i, j):
        return jnp.clip(jnp.where(j == 0, i - 1, i), 0, n_tiles - 1)

    return pl.pallas_call(
        kernel,
        grid=(n_tiles + 1, nf),
        in_specs=[
            pl.BlockSpec((tm, d), lambda i, j: (jnp.minimum(i, n_tiles - 1), 0)),
            pl.BlockSpec((1, d), lambda i, j: (0, 0)),
            pl.BlockSpec((None, d, fc), lambda i, j: (layer, 0, up_chunk(i, j))),
            pl.BlockSpec((None, d, fc), lambda i, j: (layer, 0, nf + up_chunk(i, j))),
            pl.BlockSpec((None, CONV_WIDTH, fc), lambda i, j: (layer, 0, up_chunk(i, j))),
            pl.BlockSpec((None, CONV_WIDTH, fc), lambda i, j: (layer, 0, nf + up_chunk(i, j))),
            pl.BlockSpec((None, 1, fc), lambda i, j: (layer, 0, up_chunk(i, j))),
            pl.BlockSpec((None, 1, fc), lambda i, j: (layer, 0, nf + up_chunk(i, j))),
            pl.BlockSpec((None, fc, d), lambda i, j: (layer, down_chunk(i, j), 0)),
        ],
        out_specs=pl.BlockSpec((tm, d), lambda i, j: (down_tile(i, j), 0)),
        out_shape=jax.ShapeDtypeStruct((t, d), F32),
        scratch_shapes=[
            pltpu.VMEM((tm, d), BF16),
            pltpu.VMEM((tm, fc), BF16),
            pltpu.VMEM((tm, fc), BF16),
            pltpu.VMEM((nf, SCAN_ROWS, fc), F32),
            pltpu.VMEM((nf, SCAN_ROWS, fc), F32),
        ],
        compiler_params=_params("arbitrary", "arbitrary"),
        name="conv_ffn",
    )(x2d, g.reshape(1, d), w_up, w_up, conv_w, conv_w, conv_b, conv_b, w_down)


def _pool_kernel(x_ref, halo_ref, g_ref, w_ref, b_ref, s_ref, o_ref, hbuf, sum_a, sum_b,
                 *, tiles_per_seq):
    i = pl.program_id(0)
    tm, d = x_ref.shape
    dg = d // len(POOL_WINDOWS)
    x = x_ref[...]
    g = g_ref[...]
    h = _rms(x, g)
    tile_in_seq = i % tiles_per_seq
    halo = jnp.where(tile_in_seq == 0, 0.0, _rms(halo_ref[...], g))
    hbuf[0:POOL_HALO, :] = halo
    hbuf[POOL_HALO:, :] = h
    t_in_seq = tile_in_seq * tm + lax.broadcasted_iota(jnp.int32, (tm, dg), 0)
    for gi, win in enumerate(POOL_WINDOWS):
        cols = slice(gi * dg, (gi + 1) * dg)
        levels = win.bit_length() - 1
        src, src_cols = hbuf, cols
        for k in range(1, levels + 1):
            lag = 1 << (k - 1)
            start = POOL_HALO - SCAN_ROWS * (levels - k)
            rows = POOL_HALO + tm - start
            acc = src[start:start + rows, src_cols] + src[start - lag:start - lag + rows, src_cols]
            if k < levels:
                src, src_cols = (sum_a, sum_b)[k % 2], slice(0, dg)
                src[start:start + rows, :] = acc
        cnt = jnp.minimum(t_in_seq + 1, win).astype(F32)
        pooled = acc / cnt - h[:, cols]
        y = _dot(pooled.astype(BF16), w_ref[gi])
        o_ref[:, cols] = x[:, cols] + (y + b_ref[:, cols]) * s_ref[:, cols]


def _pool_mixer(x2d, g, w, b, scale, *, seq, tm=512):
    t, d = x2d.shape
    ng, dg, _ = w.shape
    assert t % tm == 0 and seq % tm == 0 and tm % POOL_HALO == 0
    halo_blocks_per_tile = tm // POOL_HALO
    kernel = functools.partial(_pool_kernel, tiles_per_seq=seq // tm)
    return pl.pallas_call(
        kernel,
        grid=(t // tm,),
        in_specs=[
            pl.BlockSpec((tm, d), lambda i: (i, 0)),
            pl.BlockSpec((POOL_HALO, d), lambda i: (jnp.maximum(i * halo_blocks_per_tile - 1, 0), 0)),
            pl.BlockSpec((1, d), lambda i: (0, 0)),
            pl.BlockSpec((ng, dg, dg), lambda i: (0, 0, 0)),
            pl.BlockSpec((1, d), lambda i: (0, 0)),
            pl.BlockSpec((1, d), lambda i: (0, 0)),
        ],
        out_specs=pl.BlockSpec((tm, d), lambda i: (i, 0)),
        out_shape=jax.ShapeDtypeStruct((t, d), F32),
        scratch_shapes=[pltpu.VMEM((POOL_HALO + tm, d), F32),
                        pltpu.VMEM((POOL_HALO + tm, dg), F32),
                        pltpu.VMEM((POOL_HALO + tm, dg), F32)],
        compiler_params=_params("arbitrary"),
        name="pool_mixer",
    )(x2d, x2d, g.reshape(1, d), w, b.reshape(1, d), scale.reshape(1, d))


def _qkv_kernel(x_ref, g_ref, w_ref, gain_ref, o_ref, h_scr, *, norm_tiles, sub):
    j = pl.program_id(1)
    tn = w_ref.shape[1]

    @pl.when(j == 0)
    def _():
        h_scr[...] = _rms(x_ref[...], g_ref[...]).astype(BF16)

    h = h_scr[...]
    normed = j < norm_tiles
    accs = [_dot(h, w_ref[:, c * sub:(c + 1) * sub]) for c in range(tn // sub)]
    for c, acc in enumerate(accs):
        for hh in range(sub // SB_HEAD_DIM):
            head = acc[:, hh * SB_HEAD_DIM:(hh + 1) * SB_HEAD_DIM]
            cols = slice(c * sub + hh * SB_HEAD_DIM, c * sub + (hh + 1) * SB_HEAD_DIM)
            o_ref[:, cols] = jnp.where(normed, _rms(head, gain_ref[:, cols]), head).astype(BF16)


def _qkv_proj(x2d, g, w_qkv, gain_row, *, tm=1024, tn=1024, sub=512):
    t, d = x2d.shape
    n = w_qkv.shape[1]
    assert t % tm == 0 and n % tn == 0 and (2 * d) % tn == 0 and sub % SB_HEAD_DIM == 0
    assert tn % sub == 0
    kernel = functools.partial(_qkv_kernel, norm_tiles=2 * d // tn, sub=sub)
    return pl.pallas_call(
        kernel,
        grid=(t // tm, n // tn),
        in_specs=[
            pl.BlockSpec((tm, d), lambda i, j: (i, 0)),
            pl.BlockSpec((1, d), lambda i, j: (0, 0)),
            pl.BlockSpec((d, tn), lambda i, j: (0, j)),
            pl.BlockSpec((1, tn), lambda i, j: (0, j)),
        ],
        out_specs=pl.BlockSpec((tm, tn), lambda i, j: (i, j)),
        out_shape=jax.ShapeDtypeStruct((t, n), BF16),
        scratch_shapes=[pltpu.VMEM((tm, d), BF16)],
        compiler_params=_params("arbitrary", "arbitrary"),
        name="sb_qkv",
    )(x2d, g.reshape(1, d), w_qkv, gain_row)


def _sb_attn_kernel(q_ref, k_ref, v_ref, o_ref, *, chain_rows):
    qi = pl.program_id(2)
    blk = q_ref.shape[0]
    dh = SB_HEAD_DIM
    n_heads = q_ref.shape[1] // dh
    row = lax.broadcasted_iota(jnp.int32, (blk, blk), 0)
    col = lax.broadcasted_iota(jnp.int32, (blk, blk), 1)
    later = jnp.where(row > col, 1.0, 0.0).astype(BF16)
    later2 = jnp.concatenate([later, later], axis=0)
    row_g = lax.broadcasted_iota(jnp.int32, (chain_rows, blk), 0)
    col_g = lax.broadcasted_iota(jnp.int32, (chain_rows, blk), 1)

    chains = [(hh, r0) for hh in range(n_heads) for r0 in range(0, blk, chain_rows)]

    def key_block(kb, state, diagonal):
        start = pl.multiple_of(kb * blk, blk)
        cols = [slice(hh * dh, (hh + 1) * dh) for hh, _ in chains]
        rows = [slice(r0, r0 + chain_rows) for _, r0 in chains]
        zs = [lax.dot_general(q_ref[r, c], k_ref[pl.ds(start, blk), c], (((1,), (1,)), ((), ())),
                              preferred_element_type=F32) for r, c in zip(rows, cols)]
        log_betas, log_1m_betas, splits = [], [], []
        for (_, r0), z in zip(chains, zs):
            log_beta = jnp.minimum(z, 0.0) - jnp.log2(1.0 + jnp.exp2(-jnp.abs(z)))
            log_1m_beta = log_beta - z
            if diagonal:
                log_1m_beta = jnp.where(col_g < row_g + r0, log_1m_beta, 0.0)
            hi = log_1m_beta.astype(BF16)
            lo = (log_1m_beta - hi.astype(F32)).astype(BF16)
            log_betas.append(log_beta)
            log_1m_betas.append(log_1m_beta)
            splits.append(jnp.concatenate([hi, lo], axis=1))
        remains = [_dot(s, later2) for s in splits]
        ws = []
        for (_, r0), log_beta, remain, (carry, _) in zip(chains, log_betas, remains, state):
            w = jnp.exp2(log_beta + remain + carry)
            if diagonal:
                w = jnp.where(col_g < row_g + r0, w, 0.0)
            ws.append(w.astype(BF16))
        out = []
        for c, w, log_1m_beta, (carry, acc) in zip(cols, ws, log_1m_betas, state):
            out.append((carry + jnp.sum(log_1m_beta, axis=-1, keepdims=True),
                        acc + _dot(w, v_ref[pl.ds(start, blk), c])))
        return tuple(out)

    zero = (jnp.zeros((chain_rows, 1), F32), jnp.zeros((chain_rows, dh), F32))
    state = key_block(qi, (zero,) * len(chains), True)
    state = lax.fori_loop(0, qi, lambda n, st: key_block(qi - 1 - n, st, False), state)
    for (hh, r0), (_, acc) in zip(chains, state):
        o_ref[r0:r0 + chain_rows, hh * dh:(hh + 1) * dh] = acc.astype(BF16)


def _sb_attention(qkv3d, *, blk=256, heads_per_step=4, chain_rows=256):
    bsz, seq, n3 = qkv3d.shape
    d = n3 // 3
    wide = heads_per_step * SB_HEAD_DIM
    groups = d // wide
    assert seq % blk == 0 and d % wide == 0 and blk % chain_rows == 0
    return pl.pallas_call(
        functools.partial(_sb_attn_kernel, chain_rows=chain_rows),
        grid=(bsz, groups, seq // blk),
        in_specs=[
            pl.BlockSpec((None, blk, wide), lambda b, h, qi: (b, qi, h)),
            pl.BlockSpec((None, seq, wide), lambda b, h, qi: (b, 0, groups + h)),
            pl.BlockSpec((None, seq, wide), lambda b, h, qi: (b, 0, 2 * groups + h)),
        ],
        out_specs=pl.BlockSpec((None, blk, wide), lambda b, h, qi: (b, qi, h)),
        out_shape=jax.ShapeDtypeStruct((bsz, seq, d), BF16),
        compiler_params=_params("arbitrary", "arbitrary", "arbitrary"),
        name="sb_attention",
    )(qkv3d, qkv3d, qkv3d)


def _proj_res_kernel(a_ref, w_ref, r_ref, o_ref, *, sub):
    a = a_ref[...]
    tn = w_ref.shape[1]
    accs = [_dot(a, w_ref[:, c * sub:(c + 1) * sub]) for c in range(tn // sub)]
    for c, acc in enumerate(accs):
        cols = slice(c * sub, (c + 1) * sub)
        o_ref[:, cols] = r_ref[:, cols] + acc


def _proj_residual(a2d, w, res2d, *, tm=1024, tn=1024, sub=512):
    t, k = a2d.shape
    n = w.shape[1]
    assert t % tm == 0 and n % tn == 0 and tn % sub == 0
    return pl.pallas_call(
        functools.partial(_proj_res_kernel, sub=sub),
        grid=(t // tm, n // tn),
        in_specs=[
            pl.BlockSpec((tm, k), lambda i, j: (i, 0)),
            pl.BlockSpec((k, tn), lambda i, j: (0, j)),
            pl.BlockSpec((tm, tn), lambda i, j: (i, j)),
        ],
        out_specs=pl.BlockSpec((tm, tn), lambda i, j: (i, j)),
        out_shape=jax.ShapeDtypeStruct((t, n), F32),
        compiler_params=_params("arbitrary", "arbitrary"),
        name="proj_residual",
    )(a2d, w, res2d)


def _rmsnorm_kernel(x_ref, g_ref, o_ref):
    o_ref[...] = _rms(x_ref[...], g_ref[...])


def _rmsnorm(x2d, g, *, tm=512):
    t, d = x2d.shape
    assert t % tm == 0
    return pl.pallas_call(
        _rmsnorm_kernel,
        grid=(t // tm,),
        in_specs=[pl.BlockSpec((tm, d), lambda i: (i, 0)), pl.BlockSpec((1, d), lambda i: (0, 0))],
        out_specs=pl.BlockSpec((tm, d), lambda i: (i, 0)),
        out_shape=jax.ShapeDtypeStruct((t, d), F32),
        compiler_params=_params("arbitrary"),
        name="rmsnorm",
    )(x2d, g.reshape(1, d))


def _s5_kernel(u_ref, wb_ref, tab_ref, pw_ref, wc_ref, dskip_ref, o_ref,
               up_scr, st_scr, xs_scr, y_scr, carry_scr):
    tc = u_ref.shape[0]
    ns = wb_ref.shape[1] // 2
    n_slabs = tc // SCAN_ROWS

    planes = up_scr.shape[0]
    for r in range(SCAN_ROWS):
        for c in range(planes):
            up_scr[c, pl.ds(r, n_slabs, stride=SCAN_ROWS), :] = (
                u_ref[r * n_slabs:(r + 1) * n_slabs, c * LANES:(c + 1) * LANES])
    u_perm = jnp.concatenate([up_scr[c] for c in range(planes)], axis=1)
    st_scr[...] = _dot(u_perm.astype(BF16), wb_ref[...])

    @pl.when(pl.program_id(2) == 0)
    def _():
        carry_scr[...] = jnp.zeros(carry_scr.shape, F32)

    ar = tab_ref[0]
    ai = tab_ref[1]

    def slab(s, x):
        xr, xi = x
        r0 = pl.multiple_of(s * SCAN_ROWS, SCAN_ROWS)
        xr, xi = (ar * xr - ai * xi + st_scr[pl.ds(r0, SCAN_ROWS), 0:ns],
                  ar * xi + ai * xr + st_scr[pl.ds(r0, SCAN_ROWS), ns:2 * ns])
        st_scr[pl.ds(r0, SCAN_ROWS), 0:ns] = xr
        st_scr[pl.ds(r0, SCAN_ROWS), ns:2 * ns] = xi
        return xr, xi

    zero = jnp.zeros((SCAN_ROWS, ns), F32)
    yr, yi = lax.fori_loop(0, n_slabs, slab, (zero, zero), unroll=2)

    cr = carry_scr[0]
    ci = carry_scr[1]
    for step, lag in enumerate((1, 2, 4)):
        mr = tab_ref[2 + 2 * step]
        mi = tab_ref[3 + 2 * step]
        sr = pltpu.roll(yr, lag, axis=0)
        si = pltpu.roll(yi, lag, axis=0)
        yr, yi = yr + mr * sr - mi * si, yi + mr * si + mi * sr
    pr = tab_ref[8]
    pi = tab_ref[9]
    yr, yi = yr + pr * cr - pi * ci, yi + pr * ci + pi * cr
    first = lax.broadcasted_iota(jnp.int32, (SCAN_ROWS, ns), 0) == 0
    init_r = jnp.where(first, cr, pltpu.roll(yr, 1, axis=0))
    init_i = jnp.where(first, ci, pltpu.roll(yi, 1, axis=0))
    last = SCAN_ROWS - 1
    carry_scr[0] = jnp.broadcast_to(yr[last:, :], (SCAN_ROWS, ns))
    carry_scr[1] = jnp.broadcast_to(yi[last:, :], (SCAN_ROWS, ns))

    def fix(s2, _):
        xr, xi = [], []
        for k in range(2):
            s = 2 * s2 + k
            r0 = pl.multiple_of(s * SCAN_ROWS, SCAN_ROWS)
            qr = pw_ref[0, pl.ds(s, 1), :]
            qi = pw_ref[1, pl.ds(s, 1), :]
            xr.append(st_scr[pl.ds(r0, SCAN_ROWS), 0:ns] + qr * init_r - qi * init_i)
            xi.append(st_scr[pl.ds(r0, SCAN_ROWS), ns:2 * ns] + qr * init_i + qi * init_r)
        r2 = pl.multiple_of(s2 * 2 * SCAN_ROWS, 2 * SCAN_ROWS)
        xs_scr[pl.ds(r2, 2 * SCAN_ROWS), 0:ns] = jnp.concatenate(xr, axis=0).astype(BF16)
        xs_scr[pl.ds(r2, 2 * SCAN_ROWS), ns:2 * ns] = jnp.concatenate(xi, axis=0).astype(BF16)
        return 0

    lax.fori_loop(0, n_slabs // 2, fix, 0)

    half_rows = tc // 2
    y_halves = [_dot(xs_scr[k * half_rows:(k + 1) * half_rows, :], wc_ref[...]) for k in range(2)]
    for k, y_half in enumerate(y_halves):
        for c in range(planes):
            y_scr[c, k * half_rows:(k + 1) * half_rows, :] = y_half[:, c * LANES:(c + 1) * LANES]
    for r in range(SCAN_ROWS):
        rows = slice(r * n_slabs, (r + 1) * n_slabs)
        for c in range(planes):
            cols = slice(c * LANES, (c + 1) * LANES)
            y = y_scr[c, pl.ds(r, n_slabs, stride=SCAN_ROWS), :] + dskip_ref[:, cols] * u_ref[rows, cols]
            o_ref[rows, cols] = jax.nn.gelu(y, approximate=True).astype(BF16)


def _s5_core(h2d, wb, tab, pw, wc, d_skip, *, bsz, seq):
    t, d = h2d.shape
    nblk, cb, ns2 = wb.shape
    n_slabs = pw.shape[2]
    tc = n_slabs * SCAN_ROWS
    assert seq % tc == 0 and n_slabs % 2 == 0 and nblk * cb == d
    nt = seq // tc
    return pl.pallas_call(
        _s5_kernel,
        grid=(bsz, nblk, nt),
        in_specs=[
            pl.BlockSpec((tc, cb), lambda b, gb, tt: (b * nt + tt, gb)),
            pl.BlockSpec((None, cb, ns2), lambda b, gb, tt: (gb, 0, 0)),
            pl.BlockSpec((None, 10, SCAN_ROWS, ns2 // 2), lambda b, gb, tt: (gb, 0, 0, 0)),
            pl.BlockSpec((None, 2, n_slabs, ns2 // 2), lambda b, gb, tt: (gb, 0, 0, 0)),
            pl.BlockSpec((None, ns2, cb), lambda b, gb, tt: (gb, 0, 0)),
            pl.BlockSpec((1, cb), lambda b, gb, tt: (0, gb)),
        ],
        out_specs=pl.BlockSpec((tc, cb), lambda b, gb, tt: (b * nt + tt, gb)),
        out_shape=jax.ShapeDtypeStruct((t, d), BF16),
        scratch_shapes=[
            pltpu.VMEM((cb // LANES, tc, LANES), F32),
            pltpu.VMEM((tc, ns2), F32),
            pltpu.VMEM((tc, ns2), BF16),
            pltpu.VMEM((cb // LANES, tc, LANES), F32),
            pltpu.VMEM((2, SCAN_ROWS, ns2 // 2), F32),
        ],
        compiler_params=_params("arbitrary", "arbitrary", "arbitrary"),
        name="s5_scan",
    )(h2d, wb, tab, pw, wc, d_skip.reshape(1, d))


def _glu_res_kernel(a_ref, wv_ref, wg_ref, bv_ref, bg_ref, r_ref, o_ref, *, sub):
    a = a_ref[...]
    tn = wv_ref.shape[1]
    pairs = []
    for c in range(tn // sub):
        cols = slice(c * sub, (c + 1) * sub)
        pairs.append((cols, _dot(a, wv_ref[:, cols]), _dot(a, wg_ref[:, cols])))
    for cols, val, gate in pairs:
        o_ref[:, cols] = r_ref[:, cols] + (val + bv_ref[:, cols]) * jax.nn.sigmoid(gate + bg_ref[:, cols])


def _glu_residual(a2d, w, b, res2d, *, tm=1024, tn=1024, sub=256):
    t, k = a2d.shape
    n = w.shape[1] // 2
    assert t % tm == 0 and n % tn == 0 and tn % sub == 0
    nj = n // tn
    return pl.pallas_call(
        functools.partial(_glu_res_kernel, sub=sub),
        grid=(t // tm, nj),
        in_specs=[
            pl.BlockSpec((tm, k), lambda i, j: (i, 0)),
            pl.BlockSpec((k, tn), lambda i, j: (0, j)),
            pl.BlockSpec((k, tn), lambda i, j: (0, nj + j)),
            pl.BlockSpec((1, tn), lambda i, j: (0, j)),
            pl.BlockSpec((1, tn), lambda i, j: (0, nj + j)),
            pl.BlockSpec((tm, tn), lambda i, j: (i, j)),
        ],
        out_specs=pl.BlockSpec((tm, tn), lambda i, j: (i, j)),
        out_shape=jax.ShapeDtypeStruct((t, n), F32),
        compiler_params=_params("arbitrary", "arbitrary"),
        name="glu_residual",
    )(a2d, w, w, b.reshape(1, -1), b.reshape(1, -1), res2d)


def _s5_tables(lam_re, lam_im, log_step, b_re, b_im, c_re, c_im, *, n_slabs):
    groups, p = lam_re.shape
    hc = b_re.shape[2]
    gpb = SSM_GROUPS_PER_BLOCK
    nblk = groups // gpb
    ns = gpb * p
    step = jnp.exp(log_step)[:, None]
    log_a_re = lam_re * step
    log_a_im = lam_im * step
    mag = jnp.exp(log_a_re)
    a_re = mag * jnp.cos(log_a_im)
    a_im = mag * jnp.sin(log_a_im)
    den = lam_re * lam_re + lam_im * lam_im
    f_re = ((a_re - 1.0) * lam_re + a_im * lam_im) / den
    f_im = (a_im * lam_re - (a_re - 1.0) * lam_im) / den
    bb_re = f_re[..., None] * b_re - f_im[..., None] * b_im
    bb_im = f_re[..., None] * b_im + f_im[..., None] * b_re

    same_group = (jnp.arange(gpb * hc)[:, None] // hc) == (jnp.arange(ns)[None, :] // p)

    def block_diag_in(bb):
        rows = bb.reshape(nblk, gpb, p, hc).transpose(0, 3, 1, 2).reshape(nblk, hc, ns)
        return jnp.where(same_group, jnp.tile(rows, (1, gpb, 1)), 0.0)

    def block_diag_out(c):
        cols = c.reshape(nblk, gpb, hc, p).transpose(0, 1, 3, 2).reshape(nblk, ns, hc)
        return jnp.where(same_group.T, jnp.tile(cols, (1, 1, gpb)), 0.0)

    wb = jnp.concatenate([block_diag_in(bb_re), block_diag_in(bb_im)], axis=2).astype(BF16)
    wc = jnp.concatenate([block_diag_out(c_re), -block_diag_out(c_im)], axis=1).astype(BF16)

    def powers(exponents):
        k = exponents.astype(F32)[None, :, None]
        k_re = k * log_a_re.reshape(nblk, 1, ns)
        k_im = k * log_a_im.reshape(nblk, 1, ns)
        return jnp.exp(k_re) * jnp.cos(k_im), jnp.exp(k_re) * jnp.sin(k_im)

    pw = jnp.stack(powers(jnp.arange(1, n_slabs + 1)), axis=1)
    lr, li = powers(n_slabs * jnp.arange(1, SCAN_ROWS + 1))
    rows = jnp.arange(SCAN_ROWS)[None, :, None]
    ones = jnp.ones((1, SCAN_ROWS, 1), F32)
    tabs = [a_re.reshape(nblk, 1, ns) * ones, a_im.reshape(nblk, 1, ns) * ones]
    for lag in (1, 2, 4):
        for z in (lr[:, lag - 1], li[:, lag - 1]):
            tabs.append(jnp.where(rows >= lag, z[:, None, :], 0.0))
    tabs += [lr, li]
    tab = jnp.stack(tabs, axis=1)
    return wb, tab, pw, wc


def kernel(x, norm_mix_g, norm_ffn_g, pool_w, pool_b, pool_scale, sb_w_qkv, sb_q_gain, sb_k_gain, sb_w_o, ssm_lam_re, ssm_lam_im, ssm_log_step, ssm_b_re, ssm_b_im, ssm_c_re, ssm_c_im, ssm_d, ssm_w_glu, ssm_b_glu, ffn_w_up, ffn_conv_w, ffn_conv_b, ffn_w_down):
    bsz, seq, d = x.shape
    depth = norm_mix_g.shape[0]
    heads = d // SB_HEAD_DIM
    x2d = x.reshape(bsz * seq, d)
    ffn_w_up = ffn_w_up.astype(BF16)
    ffn_w_down = ffn_w_down.astype(BF16)
    ffn_conv_b = ffn_conv_b.reshape(depth, 1, -1)
    for i in range(depth):
        kind = i % 3
        j = i // 3
        if kind == 0:
            x2d = _pool_mixer(x2d, norm_mix_g[i], pool_w[j].astype(BF16), pool_b[j], pool_scale[j],
                              seq=seq)
        elif kind == 1:
            gain_row = jnp.concatenate([
                jnp.tile(sb_q_gain[j] * (math.log2(math.e) / math.sqrt(SB_HEAD_DIM)), heads),
                jnp.tile(sb_k_gain[j], heads),
                jnp.ones((d,), F32)]).reshape(1, 3 * d)
            qkv = _qkv_proj(x2d, norm_mix_g[i], sb_w_qkv[j].astype(BF16), gain_row)
            o = _sb_attention(qkv.reshape(bsz, seq, 3 * d))
            x2d = _proj_residual(o.reshape(bsz * seq, d), sb_w_o[j].astype(BF16), x2d)
        else:
            wb, tab, pw, wc = _s5_tables(ssm_lam_re[j], ssm_lam_im[j], ssm_log_step[j],
                                         ssm_b_re[j], ssm_b_im[j], ssm_c_re[j], ssm_c_im[j],
                                         n_slabs=S5_TIME_CHUNK // SCAN_ROWS)
            h = _rmsnorm(x2d, norm_mix_g[i])
            y = _s5_core(h, wb, tab, pw, wc, ssm_d[j], bsz=bsz, seq=seq)
            x2d = _glu_residual(y, ssm_w_glu[j].astype(BF16), ssm_b_glu[j], x2d)
        x2d = _conv_ffn(x2d, norm_ffn_g[i], ffn_w_up, ffn_conv_w, ffn_conv_b, ffn_w_down,
                        layer=i, seq=seq)
    return x2d.reshape(bsz, seq, d)
```

```python
import functools
import math

import jax
import jax.numpy as jnp
from jax import lax
from jax.experimental import pallas as pl
from jax.experimental.pallas import tpu as pltpu

RMS_EPS = 1e-6
POOL_WINDOWS = (2, 4, 8, 16)
POOL_HALO = 32
SB_HEAD_DIM = 128
SSM_GROUP_CH = 16
SSM_STATE = 64
SSM_GROUPS_PER_BLOCK = 16
SCAN_ROWS = 8
S5_TIME_CHUNK = 1024
LANES = 128
CONV_WIDTH = 3

V7X_VMEM_LIMIT_BYTES = 60 * 1024 * 1024

F32 = jnp.float32
BF16 = jnp.bfloat16


def _params(*semantics):
    return pltpu.CompilerParams(dimension_semantics=semantics,
                                vmem_limit_bytes=V7X_VMEM_LIMIT_BYTES)


def _rms(x, g):
    ms = jnp.mean(x * x, axis=-1, keepdims=True)
    return x * lax.rsqrt(ms + RMS_EPS) * g


def _dot(a, b):
    return jnp.dot(a, b, preferred_element_type=F32)


def _ffn_kernel(x_ref, g_ref, wv_ref, wg_ref, cwv_ref, cwg_ref, cbv_ref, cbg_ref, wd_ref,
                o_ref, h_scr, act_a, act_b, tail_v, tail_g, *, tiles_per_seq):
    i = pl.program_id(0)
    j = pl.program_id(1)
    n_tiles = pl.num_programs(0) - 1
    nf = pl.num_programs(1)
    tm = x_ref.shape[0]
    fc = wv_ref.shape[1]
    step = i * nf + j

    @pl.when(step == 0)
    def _():
        act_b[...] = jnp.zeros(act_b.shape, BF16)
        o_ref[...] = jnp.zeros(o_ref.shape, F32)

    @pl.when((j == 0) & (i < n_tiles))
    def _():
        h_scr[...] = _rms(x_ref[...], g_ref[...]).astype(BF16)

        @pl.when(i % tiles_per_seq == 0)
        def _():
            tail_v[...] = jnp.zeros(tail_v.shape, F32)
            tail_g[...] = jnp.zeros(tail_g.shape, F32)

    @pl.when((j == 1) & (i < n_tiles))
    def _():
        o_ref[...] = x_ref[...]

    chunk = jnp.where(i < n_tiles, j, nf - 1)
    active = (i < n_tiles) | (j == 0)
    row = lax.broadcasted_iota(jnp.int32, (SCAN_ROWS, fc), 0)

    def causal_conv(up, tail_ref, cw_ref, cb_ref):
        prev = tail_ref[chunk]
        tail_ref[chunk] = up[tm - SCAN_ROWS:, :]
        out = cb_ref[...] + cw_ref[2:3, :] * up
        for lag in (1, 2):
            shifted = pltpu.roll(up, lag, axis=0)
            head = jnp.where(row < lag, pltpu.roll(prev, lag, axis=0), shifted[:SCAN_ROWS])
            shifted = jnp.concatenate([head, shifted[SCAN_ROWS:]], axis=0)
            out = out + cw_ref[2 - lag:3 - lag, :] * shifted
        return out

    def stages(act_new, act_old):
        h = h_scr[...]
        val = causal_conv(_dot(h, wv_ref[...]), tail_v, cwv_ref, cbv_ref)
        up_gate = _dot(h, wg_ref[...])
        o_ref[...] += _dot(act_old[...], wd_ref[...])
        gate = causal_conv(up_gate, tail_g, cwg_ref, cbg_ref)
        half = 0.5 * gate
        act_new[...] = ((half + half * jnp.tanh(half)) * val).astype(BF16)

    pl.when(active & (step % 2 == 0))(lambda: stages(act_a, act_b))
    pl.when(active & (step % 2 == 1))(lambda: stages(act_b, act_a))


def _conv_ffn(x2d, g, w_up, conv_w, conv_b, w_down, *, layer, seq, tm=1024, fc=512):
    t, d = x2d.shape
    f = w_down.shape[1]
    nf = f // fc
    n_tiles = t // tm
    assert t % tm == 0 and seq % tm == 0 and f % fc == 0
    kernel = functools.partial(_ffn_kernel, tiles_per_seq=seq // tm)

    def up_chunk(i, j):
        return jnp.where(i < n_tiles, j, nf - 1)

    def down_chunk(i, j):
        return jnp.where((i < n_tiles) & (j > 0), j - 1, nf - 1)

    def down_tile(i, j):
        return jnp.clip(jnp.where(j == 0, i - 1, i), 0, n_tiles - 1)

    return pl.pallas_call(
        kernel,
        grid=(n_tiles + 1, nf),
        in_specs=[
            pl.BlockSpec((tm, d), lambda i, j: (jnp.minimum(i, n_tiles - 1), 0)),
            pl.BlockSpec((1, d), lambda i, j: (0, 0)),
            pl.BlockSpec((None, d, fc), lambda i, j: (layer, 0, up_chunk(i, j))),
            pl.BlockSpec((None, d, fc), lambda i, j: (layer, 0, nf + up_chunk(i, j))),
            pl.BlockSpec((None, CONV_WIDTH, fc), lambda i, j: (layer, 0, up_chunk(i, j))),
            pl.BlockSpec((None, CONV_WIDTH, fc), lambda i, j: (layer, 0, nf + up_chunk(i, j))),
            pl.BlockSpec((None, 1, fc), lambda i, j: (layer, 0, up_chunk(i, j))),
            pl.BlockSpec((None, 1, fc), lambda i, j: (layer, 0, nf + up_chunk(i, j))),
            pl.BlockSpec((None, fc, d), lambda i, j: (layer, down_chunk(i, j), 0)),
        ],
        out_specs=pl.BlockSpec((tm, d), lambda i, j: (down_tile(i, j), 0)),
        out_shape=jax.ShapeDtypeStruct((t, d), F32),
        scratch_shapes=[
            pltpu.VMEM((tm, d), BF16),
            pltpu.VMEM((tm, fc), BF16),
            pltpu.VMEM((tm, fc), BF16),
            pltpu.VMEM((nf, SCAN_ROWS, fc), F32),
            pltpu.VMEM((nf, SCAN_ROWS, fc), F32),
        ],
        compiler_params=_params("arbitrary", "arbitrary"),
        name="conv_ffn",
    )(x2d, g.reshape(1, d), w_up, w_up, conv_w, conv_w, conv_b, conv_b, w_down)


def _pool_kernel(x_ref, halo_ref, g_ref, w_ref, b_ref, s_ref, o_ref, hbuf, sum_a, sum_b,
                 *, tiles_per_seq):
    i = pl.program_id(0)
    tm, d = x_ref.shape
    dg = d // len(POOL_WINDOWS)
    x = x_ref[...]
    g = g_ref[...]
    h = _rms(x, g)
    tile_in_seq = i % tiles_per_seq
    halo = jnp.where(tile_in_seq == 0, 0.0, _rms(halo_ref[...], g))
    hbuf[0:POOL_HALO, :] = halo
    hbuf[POOL_HALO:, :] = h
    t_in_seq = tile_in_seq * tm + lax.broadcasted_iota(jnp.int32, (tm, dg), 0)
    for gi, win in enumerate(POOL_WINDOWS):
        cols = slice(gi * dg, (gi + 1) * dg)
        levels = win.bit_length() - 1
        src, src_cols = hbuf, cols
        for k in range(1, levels + 1):
            lag = 1 << (k - 1)
            start = POOL_HALO - SCAN_ROWS * (levels - k)
            rows = POOL_HALO + tm - start
            acc = src[start:start + rows, src_cols] + src[start - lag:start - lag + rows, src_cols]
            if k < levels:
                src, src_cols = (sum_a, sum_b)[k % 2], slice(0, dg)
                src[start:start + rows, :] = acc
        cnt = jnp.minimum(t_in_seq + 1, win).astype(F32)
        pooled = acc / cnt - h[:, cols]
        y = _dot(pooled.astype(BF16), w_ref[gi])
        o_ref[:, cols] = x[:, cols] + (y + b_ref[:, cols]) * s_ref[:, cols]


def _pool_mixer(x2d, g, w, b, scale, *, seq, tm=512):
    t, d = x2d.shape
    ng, dg, _ = w.shape
    assert t % tm == 0 and seq % tm == 0 and tm % POOL_HALO == 0
    halo_blocks_per_tile = tm // POOL_HALO
    kernel = functools.partial(_pool_kernel, tiles_per_seq=seq // tm)
    return pl.pallas_call(
        kernel,
        grid=(t // tm,),
        in_specs=[
            pl.BlockSpec((tm, d), lambda i: (i, 0)),
            pl.BlockSpec((POOL_HALO, d), lambda i: (jnp.maximum(i * halo_blocks_per_tile - 1, 0), 0)),
            pl.BlockSpec((1, d), lambda i: (0, 0)),
            pl.BlockSpec((ng, dg, dg), lambda i: (0, 0, 0)),
            pl.BlockSpec((1, d), lambda i: (0, 0)),
            pl.BlockSpec((1, d), lambda i: (0, 0)),
        ],
        out_specs=pl.BlockSpec((tm, d), lambda i: (i, 0)),
        out_shape=jax.ShapeDtypeStruct((t, d), F32),
        scratch_shapes=[pltpu.VMEM((POOL_HALO + tm, d), F32),
                        pltpu.VMEM((POOL_HALO + tm, dg), F32),
                        pltpu.VMEM((POOL_HALO + tm, dg), F32)],
        compiler_params=_params("arbitrary"),
        name="pool_mixer",
    )(x2d, x2d, g.reshape(1, d), w, b.reshape(1, d), scale.reshape(1, d))


def _qkv_kernel(x_ref, g_ref, w_ref, gain_ref, o_ref, h_scr, *, norm_tiles, sub):
    j = pl.program_id(1)
    tn = w_ref.shape[1]

    @pl.when(j == 0)
    def _():
        h_scr[...] = _rms(x_ref[...], g_ref[...]).astype(BF16)

    h = h_scr[...]
    normed = j < norm_tiles
    accs = [_dot(h, w_ref[:, c * sub:(c + 1) * sub]) for c in range(tn // sub)]
    for c, acc in enumerate(accs):
        for hh in range(sub // SB_HEAD_DIM):
            head = acc[:, hh * SB_HEAD_DIM:(hh + 1) * SB_HEAD_DIM]
            cols = slice(c * sub + hh * SB_HEAD_DIM, c * sub + (hh + 1) * SB_HEAD_DIM)
            o_ref[:, cols] = jnp.where(normed, _rms(head, gain_ref[:, cols]), head).astype(BF16)


def _qkv_proj(x2d, g, w_qkv, gain_row, *, tm=1024, tn=1024, sub=512):
    t, d = x2d.shape
    n = w_qkv.shape[1]
    assert t % tm == 0 and n % tn == 0 and (2 * d) % tn == 0 and sub % SB_HEAD_DIM == 0
    assert tn % sub == 0
    kernel = functools.partial(_qkv_kernel, norm_tiles=2 * d // tn, sub=sub)
    return pl.pallas_call(
        kernel,
        grid=(t // tm, n // tn),
        in_specs=[
            pl.BlockSpec((tm, d), lambda i, j: (i, 0)),
            pl.BlockSpec((1, d), lambda i, j: (0, 0)),
            pl.BlockSpec((d, tn), lambda i, j: (0, j)),
            pl.BlockSpec((1, tn), lambda i, j: (0, j)),
        ],
        out_specs=pl.BlockSpec((tm, tn), lambda i, j: (i, j)),
        out_shape=jax.ShapeDtypeStruct((t, n), BF16),
        scratch_shapes=[pltpu.VMEM((tm, d), BF16)],
        compiler_params=_params("arbitrary", "arbitrary"),
        name="sb_qkv",
    )(x2d, g.reshape(1, d), w_qkv, gain_row)


def _sb_attn_kernel(q_ref, k_ref, v_ref, o_ref, *, chain_rows):
    qi = pl.program_id(2)
    blk = q_ref.shape[0]
    dh = SB_HEAD_DIM
    n_heads = q_ref.shape[1] // dh
    row = lax.broadcasted_iota(jnp.int32, (blk, blk), 0)
    col = lax.broadcasted_iota(jnp.int32, (blk, blk), 1)
    later = jnp.where(row > col, 1.0, 0.0).astype(BF16)
    later2 = jnp.concatenate([later, later], axis=0)
    row_g = lax.broadcasted_iota(jnp.int32, (chain_rows, blk), 0)
    col_g = lax.broadcasted_iota(jnp.int32, (chain_rows, blk), 1)

    chains = [(hh, r0) for hh in range(n_heads) for r0 in range(0, blk, chain_rows)]

    def key_block(kb, state, diagonal):
        start = pl.multiple_of(kb * blk, blk)
        cols = [slice(hh * dh, (hh + 1) * dh) for hh, _ in chains]
        rows = [slice(r0, r0 + chain_rows) for _, r0 in chains]
        zs = [lax.dot_general(q_ref[r, c], k_ref[pl.ds(start, blk), c], (((1,), (1,)), ((), ())),
                              preferred_element_type=F32) for r, c in zip(rows, cols)]
        log_betas, log_1m_betas, splits = [], [], []
        for (_, r0), z in zip(chains, zs):
            log_beta = jnp.minimum(z, 0.0) - jnp.log2(1.0 + jnp.exp2(-jnp.abs(z)))
            log_1m_beta = log_beta - z
            if diagonal:
                log_1m_beta = jnp.where(col_g < row_g + r0, log_1m_beta, 0.0)
            hi = log_1m_beta.astype(BF16)
            lo = (log_1m_beta - hi.astype(F32)).astype(BF16)
            log_betas.append(log_beta)
            log_1m_betas.append(log_1m_beta)
            splits.append(jnp.concatenate([hi, lo], axis=1))
        remains = [_dot(s, later2) for s in splits]
        ws = []
        for (_, r0), log_beta, remain, (carry, _) in zip(chains, log_betas, remains, state):
            w = jnp.exp2(log_beta + remain + carry)
            if diagonal:
                w = jnp.where(col_g < row_g + r0, w, 0.0)
            ws.append(w.astype(BF16))
        out = []
        for c, w, log_1m_beta, (carry, acc) in zip(cols, ws, log_1m_betas, state):
            out.append((carry + jnp.sum(log_1m_beta, axis=-1, keepdims=True),
                        acc + _dot(w, v_ref[pl.ds(start, blk), c])))
        return tuple(out)

    zero = (jnp.zeros((chain_rows, 1), F32), jnp.zeros((chain_rows, dh), F32))
    state = key_block(qi, (zero,) * len(chains), True)
    state = lax.fori_loop(0, qi, lambda n, st: key_block(qi - 1 - n, st, False), state)
    for (hh, r0), (_, acc) in zip(chains, state):
        o_ref[r0:r0 + chain_rows, hh * dh:(hh + 1) * dh] = acc.astype(BF16)


def _sb_attention(qkv3d, *, blk=256, heads_per_step=4, chain_rows=256):
    bsz, seq, n3 = qkv3d.shape
    d = n3 // 3
    wide = heads_per_step * SB_HEAD_DIM
    groups = d // wide
    assert seq % blk == 0 and d % wide == 0 and blk % chain_rows == 0
    return pl.pallas_call(
        functools.partial(_sb_attn_kernel, chain_rows=chain_rows),
        grid=(bsz, groups, seq // blk),
        in_specs=[
            pl.BlockSpec((None, blk, wide), lambda b, h, qi: (b, qi, h)),
            pl.BlockSpec((None, seq, wide), lambda b, h, qi: (b, 0, groups + h)),
            pl.BlockSpec((None, seq, wide), lambda b, h, qi: (b, 0, 2 * groups + h)),
        ],
        out_specs=pl.BlockSpec((None, blk, wide), lambda b, h, qi: (b, qi, h)),
        out_shape=jax.ShapeDtypeStruct((bsz, seq, d), BF16),
        compiler_params=_params("arbitrary", "arbitrary", "arbitrary"),
        name="sb_attention",
    )(qkv3d, qkv3d, qkv3d)


def _proj_res_kernel(a_ref, w_ref, r_ref, o_ref, *, sub):
    a = a_ref[...]
    tn = w_ref.shape[1]
    accs = [_dot(a, w_ref[:, c * sub:(c + 1) * sub]) for c in range(tn // sub)]
    for c, acc in enumerate(accs):
        cols = slice(c * sub, (c + 1) * sub)
        o_ref[:, cols] = r_ref[:, cols] + acc


def _proj_residual(a2d, w, res2d, *, tm=1024, tn=1024, sub=512):
    t, k = a2d.shape
    n = w.shape[1]
    assert t % tm == 0 and n % tn == 0 and tn % sub == 0
    return pl.pallas_call(
        functools.partial(_proj_res_kernel, sub=sub),
        grid=(t // tm, n // tn),
        in_specs=[
            pl.BlockSpec((tm, k), lambda i, j: (i, 0)),
            pl.BlockSpec((k, tn), lambda i, j: (0, j)),
            pl.BlockSpec((tm, tn), lambda i, j: (i, j)),
        ],
        out_specs=pl.BlockSpec((tm, tn), lambda i, j: (i, j)),
        out_shape=jax.ShapeDtypeStruct((t, n), F32),
        compiler_params=_params("arbitrary", "arbitrary"),
        name="proj_residual",
    )(a2d, w, res2d)


def _inv_rms_kernel(x_ref, o_ref):
    x = x_ref[...]
    inv = lax.rsqrt(jnp.mean(x * x, axis=-1, keepdims=True) + RMS_EPS)
    o_ref[...] = jnp.broadcast_to(inv, o_ref.shape)


def _inv_rms(x2d, *, tm=512):
    t, d = x2d.shape
    assert t % tm == 0
    return pl.pallas_call(
        _inv_rms_kernel,
        grid=(t // tm,),
        in_specs=[pl.BlockSpec((tm, d), lambda i: (i, 0))],
        out_specs=pl.BlockSpec((tm, LANES), lambda i: (i, 0)),
        out_shape=jax.ShapeDtypeStruct((t, LANES), F32),
        compiler_params=_params("arbitrary"),
        name="inv_rms",
    )(x2d)


def _s5_kernel(x_ref, inv_ref, g_ref, wb_ref, tab_ref, pw_ref, wc_ref, dskip_ref, o_ref,
               up_scr, st_scr, xs_scr, y_scr, carry_scr):
    tc = x_ref.shape[0]
    ns = wb_ref.shape[1] // 2
    n_slabs = tc // SCAN_ROWS

    def u_block(rows, cols):
        return x_ref[rows, cols] * inv_ref[rows, :] * g_ref[:, cols]

    planes = up_scr.shape[0]
    for r in range(SCAN_ROWS):
        for c in range(planes):
            up_scr[c, pl.ds(r, n_slabs, stride=SCAN_ROWS), :] = u_block(
                slice(r * n_slabs, (r + 1) * n_slabs), slice(c * LANES, (c + 1) * LANES))
    u_perm = jnp.concatenate([up_scr[c] for c in range(planes)], axis=1)
    st_scr[...] = _dot(u_perm.astype(BF16), wb_ref[...])

    @pl.when(pl.program_id(2) == 0)
    def _():
        carry_scr[...] = jnp.zeros(carry_scr.shape, F32)

    ar = tab_ref[0]
    ai = tab_ref[1]

    def slab(s, x):
        xr, xi = x
        r0 = pl.multiple_of(s * SCAN_ROWS, SCAN_ROWS)
        xr, xi = (ar * xr - ai * xi + st_scr[pl.ds(r0, SCAN_ROWS), 0:ns],
                  ar * xi + ai * xr + st_scr[pl.ds(r0, SCAN_ROWS), ns:2 * ns])
        st_scr[pl.ds(r0, SCAN_ROWS), 0:ns] = xr
        st_scr[pl.ds(r0, SCAN_ROWS), ns:2 * ns] = xi
        return xr, xi

    zero = jnp.zeros((SCAN_ROWS, ns), F32)
    yr, yi = lax.fori_loop(0, n_slabs, slab, (zero, zero), unroll=2)

    cr = carry_scr[0]
    ci = carry_scr[1]
    for step, lag in enumerate((1, 2, 4)):
        mr = tab_ref[2 + 2 * step]
        mi = tab_ref[3 + 2 * step]
        sr = pltpu.roll(yr, lag, axis=0)
        si = pltpu.roll(yi, lag, axis=0)
        yr, yi = yr + mr * sr - mi * si, yi + mr * si + mi * sr
    pr = tab_ref[8]
    pi = tab_ref[9]
    yr, yi = yr + pr * cr - pi * ci, yi + pr * ci + pi * cr
    first = lax.broadcasted_iota(jnp.int32, (SCAN_ROWS, ns), 0) == 0
    init_r = jnp.where(first, cr, pltpu.roll(yr, 1, axis=0))
    init_i = jnp.where(first, ci, pltpu.roll(yi, 1, axis=0))
    last = SCAN_ROWS - 1
    carry_scr[0] = jnp.broadcast_to(yr[last:, :], (SCAN_ROWS, ns))
    carry_scr[1] = jnp.broadcast_to(yi[last:, :], (SCAN_ROWS, ns))

    def fix(s2, _):
        xr, xi = [], []
        for k in range(2):
            s = 2 * s2 + k
            r0 = pl.multiple_of(s * SCAN_ROWS, SCAN_ROWS)
            qr = pw_ref[0, pl.ds(s, 1), :]
            qi = pw_ref[1, pl.ds(s, 1), :]
            xr.append(st_scr[pl.ds(r0, SCAN_ROWS), 0:ns] + qr * init_r - qi * init_i)
            xi.append(st_scr[pl.ds(r0, SCAN_ROWS), ns:2 * ns] + qr * init_i + qi * init_r)
        r2 = pl.multiple_of(s2 * 2 * SCAN_ROWS, 2 * SCAN_ROWS)
        xs_scr[pl.ds(r2, 2 * SCAN_ROWS), 0:ns] = jnp.concatenate(xr, axis=0).astype(BF16)
        xs_scr[pl.ds(r2, 2 * SCAN_ROWS), ns:2 * ns] = jnp.concatenate(xi, axis=0).astype(BF16)
        return 0

    lax.fori_loop(0, n_slabs // 2, fix, 0)

    half_rows = tc // 2
    y_halves = [_dot(xs_scr[k * half_rows:(k + 1) * half_rows, :], wc_ref[...]) for k in range(2)]
    for k, y_half in enumerate(y_halves):
        for c in range(planes):
            y_scr[c, k * half_rows:(k + 1) * half_rows, :] = y_half[:, c * LANES:(c + 1) * LANES]
    for r in range(SCAN_ROWS):
        rows = slice(r * n_slabs, (r + 1) * n_slabs)
        for c in range(planes):
            cols = slice(c * LANES, (c + 1) * LANES)
            y = y_scr[c, pl.ds(r, n_slabs, stride=SCAN_ROWS), :] + dskip_ref[:, cols] * u_block(rows, cols)
            o_ref[rows, cols] = jax.nn.gelu(y, approximate=True).astype(BF16)


def _s5_core(x2d, inv_rms, g, wb, tab, pw, wc, d_skip, *, bsz, seq):
    t, d = x2d.shape
    nblk, cb, ns2 = wb.shape
    n_slabs = pw.shape[2]
    tc = n_slabs * SCAN_ROWS
    assert seq % tc == 0 and n_slabs % 2 == 0 and nblk * cb == d
    nt = seq // tc
    return pl.pallas_call(
        _s5_kernel,
        grid=(bsz, nblk, nt),
        in_specs=[
            pl.BlockSpec((tc, cb), lambda b, gb, tt: (b * nt + tt, gb)),
            pl.BlockSpec((tc, LANES), lambda b, gb, tt: (b * nt + tt, 0)),
            pl.BlockSpec((1, cb), lambda b, gb, tt: (0, gb)),
            pl.BlockSpec((None, cb, ns2), lambda b, gb, tt: (gb, 0, 0)),
            pl.BlockSpec((None, 10, SCAN_ROWS, ns2 // 2), lambda b, gb, tt: (gb, 0, 0, 0)),
            pl.BlockSpec((None, 2, n_slabs, ns2 // 2), lambda b, gb, tt: (gb, 0, 0, 0)),
            pl.BlockSpec((None, ns2, cb), lambda b, gb, tt: (gb, 0, 0)),
            pl.BlockSpec((1, cb), lambda b, gb, tt: (0, gb)),
        ],
        out_specs=pl.BlockSpec((tc, cb), lambda b, gb, tt: (b * nt + tt, gb)),
        out_shape=jax.ShapeDtypeStruct((t, d), BF16),
        scratch_shapes=[
            pltpu.VMEM((cb // LANES, tc, LANES), F32),
            pltpu.VMEM((tc, ns2), F32),
            pltpu.VMEM((tc, ns2), BF16),
            pltpu.VMEM((cb // LANES, tc, LANES), F32),
            pltpu.VMEM((2, SCAN_ROWS, ns2 // 2), F32),
        ],
        compiler_params=_params("arbitrary", "arbitrary", "arbitrary"),
        name="s5_scan",
    )(x2d, inv_rms, g.reshape(1, d), wb, tab, pw, wc, d_skip.reshape(1, d))


def _glu_res_kernel(a_ref, wv_ref, wg_ref, bv_ref, bg_ref, r_ref, o_ref, *, sub):
    a = a_ref[...]
    tn = wv_ref.shape[1]
    pairs = []
    for c in range(tn // sub):
        cols = slice(c * sub, (c + 1) * sub)
        pairs.append((cols, _dot(a, wv_ref[:, cols]), _dot(a, wg_ref[:, cols])))
    for cols, val, gate in pairs:
        o_ref[:, cols] = r_ref[:, cols] + (val + bv_ref[:, cols]) * jax.nn.sigmoid(gate + bg_ref[:, cols])


def _glu_residual(a2d, w, b, res2d, *, tm=1024, tn=1024, sub=256):
    t, k = a2d.shape
    n = w.shape[1] // 2
    assert t % tm == 0 and n % tn == 0 and tn % sub == 0
    nj = n // tn
    return pl.pallas_call(
        functools.partial(_glu_res_kernel, sub=sub),
        grid=(t // tm, nj),
        in_specs=[
            pl.BlockSpec((tm, k), lambda i, j: (i, 0)),
            pl.BlockSpec((k, tn), lambda i, j: (0, j)),
            pl.BlockSpec((k, tn), lambda i, j: (0, nj + j)),
            pl.BlockSpec((1, tn), lambda i, j: (0, j)),
            pl.BlockSpec((1, tn), lambda i, j: (0, nj + j)),
            pl.BlockSpec((tm, tn), lambda i, j: (i, j)),
        ],
        out_specs=pl.BlockSpec((tm, tn), lambda i, j: (i, j)),
        out_shape=jax.ShapeDtypeStruct((t, n), F32),
        compiler_params=_params("arbitrary", "arbitrary"),
        name="glu_residual",
    )(a2d, w, w, b.reshape(1, -1), b.reshape(1, -1), res2d)


def _s5_tables(lam_re, lam_im, log_step, b_re, b_im, c_re, c_im, *, n_slabs):
    groups, p = lam_re.shape
    hc = b_re.shape[2]
    gpb = SSM_GROUPS_PER_BLOCK
    nblk = groups // gpb
    ns = gpb * p
    step = jnp.exp(log_step)[:, None]
    log_a_re = lam_re * step
    log_a_im = lam_im * step
    mag = jnp.exp(log_a_re)
    a_re = mag * jnp.cos(log_a_im)
    a_im = mag * jnp.sin(log_a_im)
    den = lam_re * lam_re + lam_im * lam_im
    f_re = ((a_re - 1.0) * lam_re + a_im * lam_im) / den
    f_im = (a_im * lam_re - (a_re - 1.0) * lam_im) / den
    bb_re = f_re[..., None] * b_re - f_im[..., None] * b_im
    bb_im = f_re[..., None] * b_im + f_im[..., None] * b_re

    same_group = (jnp.arange(gpb * hc)[:, None] // hc) == (jnp.arange(ns)[None, :] // p)

    def group_transpose(m):
        a = m.shape[1]
        eye = jnp.broadcast_to(jnp.eye(m.shape[2], dtype=F32), (nblk, gpb) + (m.shape[2],) * 2)
        return jnp.einsum('bgkq,bgaq->bgka', eye, m.reshape(nblk, gpb, a, m.shape[2]))

    def lane_tile(m):
        copies = jnp.tile(jnp.eye(m.shape[2], dtype=F32), (1, gpb))
        return jnp.einsum('brw,wn->brn', m, copies)

    def block_diag_in(bb):
        rows = group_transpose(bb).reshape(nblk, gpb * hc, p)
        return jnp.where(same_group, lane_tile(rows), 0.0)

    def block_diag_out(c):
        cols = group_transpose(c).reshape(nblk, ns, hc)
        return jnp.where(same_group.T, lane_tile(cols), 0.0)

    wb = jnp.concatenate([block_diag_in(bb_re), block_diag_in(bb_im)], axis=2).astype(BF16)
    wc = jnp.concatenate([block_diag_out(c_re), -block_diag_out(c_im)], axis=1).astype(BF16)

    def powers(exponents):
        k = exponents.astype(F32)[None, :, None]
        k_re = k * log_a_re.reshape(nblk, 1, ns)
        k_im = k * log_a_im.reshape(nblk, 1, ns)
        return jnp.exp(k_re) * jnp.cos(k_im), jnp.exp(k_re) * jnp.sin(k_im)

    pw = jnp.stack(powers(jnp.arange(1, n_slabs + 1)), axis=1)
    lr, li = powers(n_slabs * jnp.arange(1, SCAN_ROWS + 1))
    rows = jnp.arange(SCAN_ROWS)[None, :, None]
    ones = jnp.ones((1, SCAN_ROWS, 1), F32)
    tabs = [a_re.reshape(nblk, 1, ns) * ones, a_im.reshape(nblk, 1, ns) * ones]
    for lag in (1, 2, 4):
        for z in (lr[:, lag - 1], li[:, lag - 1]):
            tabs.append(jnp.where(rows >= lag, z[:, None, :], 0.0))
    tabs += [lr, li]
    tab = jnp.stack(tabs, axis=1)
    return wb, tab, pw, wc


def kernel(x, norm_mix_g, norm_ffn_g, pool_w, pool_b, pool_scale, sb_w_qkv, sb_q_gain, sb_k_gain, sb_w_o, ssm_lam_re, ssm_lam_im, ssm_log_step, ssm_b_re, ssm_b_im, ssm_c_re, ssm_c_im, ssm_d, ssm_w_glu, ssm_b_glu, ffn_w_up, ffn_conv_w, ffn_conv_b, ffn_w_down):
    bsz, seq, d = x.shape
    depth = norm_mix_g.shape[0]
    heads = d // SB_HEAD_DIM
    x2d = x.reshape(bsz * seq, d)
    ffn_w_up = ffn_w_up.astype(BF16)
    ffn_w_down = ffn_w_down.astype(BF16)
    ffn_conv_b = ffn_conv_b.reshape(depth, 1, -1)
    for i in range(depth):
        kind = i % 3
        j = i // 3
        if kind == 0:
            x2d = _pool_mixer(x2d, norm_mix_g[i], pool_w[j].astype(BF16), pool_b[j], pool_scale[j],
                              seq=seq)
        elif kind == 1:
            gain_row = jnp.concatenate([
                jnp.tile(sb_q_gain[j] * (math.log2(math.e) / math.sqrt(SB_HEAD_DIM)), heads),
                jnp.tile(sb_k_gain[j], heads),
                jnp.ones((d,), F32)]).reshape(1, 3 * d)
            qkv = _qkv_proj(x2d, norm_mix_g[i], sb_w_qkv[j].astype(BF16), gain_row)
            o = _sb_attention(qkv.reshape(bsz, seq, 3 * d))
            x2d = _proj_residual(o.reshape(bsz * seq, d), sb_w_o[j].astype(BF16), x2d)
        else:
            wb, tab, pw, wc = _s5_tables(ssm_lam_re[j], ssm_lam_im[j], ssm_log_step[j],
                                         ssm_b_re[j], ssm_b_im[j], ssm_c_re[j], ssm_c_im[j],
                                         n_slabs=S5_TIME_CHUNK // SCAN_ROWS)
            y = _s5_core(x2d, _inv_rms(x2d), norm_mix_g[i], wb, tab, pw, wc, ssm_d[j],
                         bsz=bsz, seq=seq)
            x2d = _glu_residual(y, ssm_w_glu[j].astype(BF16), ssm_b_glu[j], x2d)
        x2d = _conv_ffn(x2d, norm_ffn_g[i], ffn_w_up, ffn_conv_w, ffn_conv_b, ffn_w_down,
                        layer=i, seq=seq)
    return x2d.reshape(bsz, seq, d)
```

```python
import functools
import math

import jax
import jax.numpy as jnp
from jax import lax
from jax.experimental import pallas as pl
from jax.experimental.pallas import tpu as pltpu

RMS_EPS = 1e-6
POOL_WINDOWS = (2, 4, 8, 16)
POOL_HALO = 32
SB_HEAD_DIM = 128
SSM_GROUP_CH = 16
SSM_STATE = 64
SSM_GROUPS_PER_BLOCK = 16
SCAN_ROWS = 8
S5_TIME_CHUNK = 1024
LANES = 128
CONV_WIDTH = 3

V7X_VMEM_LIMIT_BYTES = 60 * 1024 * 1024

F32 = jnp.float32
BF16 = jnp.bfloat16


def _params(*semantics):
    return pltpu.CompilerParams(dimension_semantics=semantics,
                                vmem_limit_bytes=V7X_VMEM_LIMIT_BYTES)


def _rms(x, g):
    ms = jnp.mean(x * x, axis=-1, keepdims=True)
    return x * lax.rsqrt(ms + RMS_EPS) * g


def _dot(a, b):
    return jnp.dot(a, b, preferred_element_type=F32)


def _ffn_kernel(x_ref, g_ref, wv_ref, wg_ref, cwv_ref, cwg_ref, cbv_ref, cbg_ref, wd_ref,
                o_ref, h_scr, act_a, act_b, tail_v, tail_g, *, tiles_per_seq):
    i = pl.program_id(0)
    j = pl.program_id(1)
    n_tiles = pl.num_programs(0) - 1
    nf = pl.num_programs(1)
    tm = x_ref.shape[0]
    fc = wv_ref.shape[1]
    step = i * nf + j

    @pl.when(step == 0)
    def _():
        act_b[...] = jnp.zeros(act_b.shape, BF16)
        o_ref[...] = jnp.zeros(o_ref.shape, F32)

    @pl.when((j == 0) & (i < n_tiles))
    def _():
        h_scr[...] = _rms(x_ref[...], g_ref[...]).astype(BF16)

        @pl.when(i % tiles_per_seq == 0)
        def _():
            tail_v[...] = jnp.zeros(tail_v.shape, F32)
            tail_g[...] = jnp.zeros(tail_g.shape, F32)

    @pl.when((j == 1) & (i < n_tiles))
    def _():
        o_ref[...] = x_ref[...]

    chunk = jnp.where(i < n_tiles, j, nf - 1)
    active = (i < n_tiles) | (j == 0)
    row = lax.broadcasted_iota(jnp.int32, (SCAN_ROWS, fc), 0)

    def causal_conv(up, tail_ref, cw_ref, cb_ref):
        prev = tail_ref[chunk]
        tail_ref[chunk] = up[tm - SCAN_ROWS:, :]
        out = cb_ref[...] + cw_ref[2:3, :] * up
        for lag in (1, 2):
            shifted = pltpu.roll(up, lag, axis=0)
            head = jnp.where(row < lag, pltpu.roll(prev, lag, axis=0), shifted[:SCAN_ROWS])
            shifted = jnp.concatenate([head, shifted[SCAN_ROWS:]], axis=0)
            out = out + cw_ref[2 - lag:3 - lag, :] * shifted
        return out

    def stages(act_new, act_old):
        h = h_scr[...]
        val = causal_conv(_dot(h, wv_ref[...]), tail_v, cwv_ref, cbv_ref)
        up_gate = _dot(h, wg_ref[...])
        o_ref[...] += _dot(act_old[...], wd_ref[...])
        gate = causal_conv(up_gate, tail_g, cwg_ref, cbg_ref)
        half = 0.5 * gate
        act_new[...] = ((half + half * jnp.tanh(half)) * val).astype(BF16)

    pl.when(active & (step % 2 == 0))(lambda: stages(act_a, act_b))
    pl.when(active & (step % 2 == 1))(lambda: stages(act_b, act_a))


def _conv_ffn(x2d, g, w_up, conv_w, conv_b, w_down, *, layer, seq, tm=1024, fc=512):
    t, d = x2d.shape
    f = w_down.shape[1]
    nf = f // fc
    n_tiles = t // tm
    assert t % tm == 0 and seq % tm == 0 and f % fc == 0
    kernel = functools.partial(_ffn_kernel, tiles_per_seq=seq // tm)

    def up_chunk(i, j):
        return jnp.where(i < n_tiles, j, nf - 1)

    def down_chunk(i, j):
        return jnp.where((i < n_tiles) & (j > 0), j - 1, nf - 1)

    def down_tile(i, j):
        return jnp.clip(jnp.where(j == 0, i - 1, i), 0, n_tiles - 1)

    return pl.pallas_call(
        kernel,
        grid=(n_tiles + 1, nf),
        in_specs=[
            pl.BlockSpec((tm, d), lambda i, j: (jnp.minimum(i, n_tiles - 1), 0)),
            pl.BlockSpec((1, d), lambda i, j: (0, 0)),
            pl.BlockSpec((None, d, fc), lambda i, j: (layer, 0, up_chunk(i, j))),
            pl.BlockSpec((None, d, fc), lambda i, j: (layer, 0, nf + up_chunk(i, j))),
            pl.BlockSpec((None, CONV_WIDTH, fc), lambda i, j: (layer, 0, up_chunk(i, j))),
            pl.BlockSpec((None, CONV_WIDTH, fc), lambda i, j: (layer, 0, nf + up_chunk(i, j))),
            pl.BlockSpec((None, 1, fc), lambda i, j: (layer, 0, up_chunk(i, j))),
            pl.BlockSpec((None, 1, fc), lambda i, j: (layer, 0, nf + up_chunk(i, j))),
            pl.BlockSpec((None, fc, d), lambda i, j: (layer, down_chunk(i, j), 0)),
        ],
        out_specs=pl.BlockSpec((tm, d), lambda i, j: (down_tile(i, j), 0)),
        out_shape=jax.ShapeDtypeStruct((t, d), F32),
        scratch_shapes=[
            pltpu.VMEM((tm, d), BF16),
            pltpu.VMEM((tm, fc), BF16),
            pltpu.VMEM((tm, fc), BF16),
            pltpu.VMEM((nf, SCAN_ROWS, fc), F32),
            pltpu.VMEM((nf, SCAN_ROWS, fc), F32),
        ],
        compiler_params=_params("arbitrary", "arbitrary"),
        name="conv_ffn",
    )(x2d, g.reshape(1, d), w_up, w_up, conv_w, conv_w, conv_b, conv_b, w_down)


def _pool_kernel(x_ref, halo_ref, g_ref, w_ref, b_ref, s_ref, o_ref, hbuf, sum_a, sum_b,
                 *, tiles_per_seq):
    i = pl.program_id(0)
    tm, d = x_ref.shape
    dg = d // len(POOL_WINDOWS)
    x = x_ref[...]
    g = g_ref[...]
    h = _rms(x, g)
    tile_in_seq = i % tiles_per_seq
    halo = jnp.where(tile_in_seq == 0, 0.0, _rms(halo_ref[...], g))
    hbuf[0:POOL_HALO, :] = halo
    hbuf[POOL_HALO:, :] = h
    head_rows = 2 * SCAN_ROWS
    assert max(POOL_WINDOWS) <= head_rows and all(w & (w - 1) == 0 for w in POOL_WINDOWS)
    t_in_seq = tile_in_seq * tm + lax.broadcasted_iota(jnp.int32, (head_rows, dg), 0)
    for gi, win in enumerate(POOL_WINDOWS):
        cols = slice(gi * dg, (gi + 1) * dg)
        levels = win.bit_length() - 1
        src, src_cols = hbuf, cols
        for k in range(1, levels + 1):
            lag = 1 << (k - 1)
            start = POOL_HALO - SCAN_ROWS * (levels - k)
            rows = POOL_HALO + tm - start
            acc = src[start:start + rows, src_cols] + src[start - lag:start - lag + rows, src_cols]
            if k < levels:
                src, src_cols = (sum_a, sum_b)[k % 2], slice(0, dg)
                src[start:start + rows, :] = acc
        hg = h[:, cols]
        head_cnt = jnp.minimum(t_in_seq + 1, win).astype(F32)
        pooled = jnp.concatenate([acc[:head_rows] / head_cnt - hg[:head_rows],
                                  acc[head_rows:] * (1.0 / win) - hg[head_rows:]], axis=0)
        y = _dot(pooled.astype(BF16), w_ref[gi])
        o_ref[:, cols] = x[:, cols] + (y + b_ref[:, cols]) * s_ref[:, cols]


def _pool_mixer(x2d, g, w, b, scale, *, seq, tm=512):
    t, d = x2d.shape
    ng, dg, _ = w.shape
    assert t % tm == 0 and seq % tm == 0 and tm % POOL_HALO == 0
    halo_blocks_per_tile = tm // POOL_HALO
    kernel = functools.partial(_pool_kernel, tiles_per_seq=seq // tm)
    return pl.pallas_call(
        kernel,
        grid=(t // tm,),
        in_specs=[
            pl.BlockSpec((tm, d), lambda i: (i, 0)),
            pl.BlockSpec((POOL_HALO, d), lambda i: (jnp.maximum(i * halo_blocks_per_tile - 1, 0), 0)),
            pl.BlockSpec((1, d), lambda i: (0, 0)),
            pl.BlockSpec((ng, dg, dg), lambda i: (0, 0, 0)),
            pl.BlockSpec((1, d), lambda i: (0, 0)),
            pl.BlockSpec((1, d), lambda i: (0, 0)),
        ],
        out_specs=pl.BlockSpec((tm, d), lambda i: (i, 0)),
        out_shape=jax.ShapeDtypeStruct((t, d), F32),
        scratch_shapes=[pltpu.VMEM((POOL_HALO + tm, d), F32),
                        pltpu.VMEM((POOL_HALO + tm, dg), F32),
                        pltpu.VMEM((POOL_HALO + tm, dg), F32)],
        compiler_params=_params("arbitrary"),
        name="pool_mixer",
    )(x2d, x2d, g.reshape(1, d), w, b.reshape(1, d), scale.reshape(1, d))


def _qkv_kernel(x_ref, g_ref, w_ref, gain_ref, o_ref, h_scr, *, norm_tiles, sub):
    j = pl.program_id(1)
    tn = w_ref.shape[1]

    @pl.when(j == 0)
    def _():
        h_scr[...] = _rms(x_ref[...], g_ref[...]).astype(BF16)

    h = h_scr[...]
    normed = j < norm_tiles
    accs = [_dot(h, w_ref[:, c * sub:(c + 1) * sub]) for c in range(tn // sub)]
    for c, acc in enumerate(accs):
        for hh in range(sub // SB_HEAD_DIM):
            head = acc[:, hh * SB_HEAD_DIM:(hh + 1) * SB_HEAD_DIM]
            cols = slice(c * sub + hh * SB_HEAD_DIM, c * sub + (hh + 1) * SB_HEAD_DIM)
            o_ref[:, cols] = jnp.where(normed, _rms(head, gain_ref[:, cols]), head).astype(BF16)


def _qkv_proj(x2d, g, w_qkv, gain_row, *, tm=1024, tn=1024, sub=512):
    t, d = x2d.shape
    n = w_qkv.shape[1]
    assert t % tm == 0 and n % tn == 0 and (2 * d) % tn == 0 and sub % SB_HEAD_DIM == 0
    assert tn % sub == 0
    kernel = functools.partial(_qkv_kernel, norm_tiles=2 * d // tn, sub=sub)
    return pl.pallas_call(
        kernel,
        grid=(t // tm, n // tn),
        in_specs=[
            pl.BlockSpec((tm, d), lambda i, j: (i, 0)),
            pl.BlockSpec((1, d), lambda i, j: (0, 0)),
            pl.BlockSpec((d, tn), lambda i, j: (0, j)),
            pl.BlockSpec((1, tn), lambda i, j: (0, j)),
        ],
        out_specs=pl.BlockSpec((tm, tn), lambda i, j: (i, j)),
        out_shape=jax.ShapeDtypeStruct((t, n), BF16),
        scratch_shapes=[pltpu.VMEM((tm, d), BF16)],
        compiler_params=_params("arbitrary", "arbitrary"),
        name="sb_qkv",
    )(x2d, g.reshape(1, d), w_qkv, gain_row)


def _sb_attn_kernel(q_ref, k_ref, v_ref, o_ref, *, chain_rows):
    qi = pl.program_id(2)
    blk = q_ref.shape[0]
    dh = SB_HEAD_DIM
    n_heads = q_ref.shape[1] // dh
    row = lax.broadcasted_iota(jnp.int32, (blk, blk), 0)
    col = lax.broadcasted_iota(jnp.int32, (blk, blk), 1)
    later = jnp.where(row > col, 1.0, 0.0).astype(BF16)
    later2 = jnp.concatenate([later, later], axis=0)
    row_g = lax.broadcasted_iota(jnp.int32, (chain_rows, blk), 0)
    col_g = lax.broadcasted_iota(jnp.int32, (chain_rows, blk), 1)

    chains = [(hh, r0) for hh in range(n_heads) for r0 in range(0, blk, chain_rows)]

    def key_block(kb, state, diagonal):
        start = pl.multiple_of(kb * blk, blk)
        cols = [slice(hh * dh, (hh + 1) * dh) for hh, _ in chains]
        rows = [slice(r0, r0 + chain_rows) for _, r0 in chains]
        zs = [lax.dot_general(q_ref[r, c], k_ref[pl.ds(start, blk), c], (((1,), (1,)), ((), ())),
                              preferred_element_type=F32) for r, c in zip(rows, cols)]
        log_betas, log_1m_betas, splits = [], [], []
        for (_, r0), z in zip(chains, zs):
            log_beta = jnp.minimum(z, 0.0) - jnp.log2(1.0 + jnp.exp2(-jnp.abs(z)))
            log_1m_beta = log_beta - z
            if diagonal:
                log_1m_beta = jnp.where(col_g < row_g + r0, log_1m_beta, 0.0)
            hi = log_1m_beta.astype(BF16)
            lo = (log_1m_beta - hi.astype(F32)).astype(BF16)
            log_betas.append(log_beta)
            log_1m_betas.append(log_1m_beta)
            splits.append(jnp.concatenate([hi, lo], axis=1))
        remains = [_dot(s, later2) for s in splits]
        ws = []
        for (_, r0), log_beta, remain, (carry, _) in zip(chains, log_betas, remains, state):
            w = jnp.exp2(log_beta + remain + carry)
            if diagonal:
                w = jnp.where(col_g < row_g + r0, w, 0.0)
            ws.append(w.astype(BF16))
        out = []
        for c, w, log_1m_beta, (carry, acc) in zip(cols, ws, log_1m_betas, state):
            out.append((carry + jnp.sum(log_1m_beta, axis=-1, keepdims=True),
                        acc + _dot(w, v_ref[pl.ds(start, blk), c])))
        return tuple(out)

    zero = (jnp.zeros((chain_rows, 1), F32), jnp.zeros((chain_rows, dh), F32))
    state = key_block(qi, (zero,) * len(chains), True)
    state = lax.fori_loop(0, qi, lambda n, st: key_block(qi - 1 - n, st, False), state)
    for (hh, r0), (_, acc) in zip(chains, state):
        o_ref[r0:r0 + chain_rows, hh * dh:(hh + 1) * dh] = acc.astype(BF16)


def _sb_attention(qkv3d, *, blk=256, heads_per_step=8, chain_rows=256):
    bsz, seq, n3 = qkv3d.shape
    d = n3 // 3
    wide = heads_per_step * SB_HEAD_DIM
    groups = d // wide
    assert seq % blk == 0 and d % wide == 0 and blk % chain_rows == 0
    return pl.pallas_call(
        functools.partial(_sb_attn_kernel, chain_rows=chain_rows),
        grid=(bsz, groups, seq // blk),
        in_specs=[
            pl.BlockSpec((None, blk, wide), lambda b, h, qi: (b, qi, h)),
            pl.BlockSpec((None, seq, wide), lambda b, h, qi: (b, 0, groups + h)),
            pl.BlockSpec((None, seq, wide), lambda b, h, qi: (b, 0, 2 * groups + h)),
        ],
        out_specs=pl.BlockSpec((None, blk, wide), lambda b, h, qi: (b, qi, h)),
        out_shape=jax.ShapeDtypeStruct((bsz, seq, d), BF16),
        compiler_params=_params("arbitrary", "arbitrary", "arbitrary"),
        name="sb_attention",
    )(qkv3d, qkv3d, qkv3d)


def _proj_res_kernel(a_ref, w_ref, r_ref, o_ref, *, sub):
    a = a_ref[...]
    tn = w_ref.shape[1]
    accs = [_dot(a, w_ref[:, c * sub:(c + 1) * sub]) for c in range(tn // sub)]
    for c, acc in enumerate(accs):
        cols = slice(c * sub, (c + 1) * sub)
        o_ref[:, cols] = r_ref[:, cols] + acc


def _proj_residual(a2d, w, res2d, *, tm=1024, tn=1024, sub=512):
    t, k = a2d.shape
    n = w.shape[1]
    assert t % tm == 0 and n % tn == 0 and tn % sub == 0
    return pl.pallas_call(
        functools.partial(_proj_res_kernel, sub=sub),
        grid=(t // tm, n // tn),
        in_specs=[
            pl.BlockSpec((tm, k), lambda i, j: (i, 0)),
            pl.BlockSpec((k, tn), lambda i, j: (0, j)),
            pl.BlockSpec((tm, tn), lambda i, j: (i, j)),
        ],
        out_specs=pl.BlockSpec((tm, tn), lambda i, j: (i, j)),
        out_shape=jax.ShapeDtypeStruct((t, n), F32),
        compiler_params=_params("arbitrary", "arbitrary"),
        name="proj_residual",
    )(a2d, w, res2d)


def _inv_rms_kernel(x_ref, o_ref):
    x = x_ref[...]
    inv = lax.rsqrt(jnp.mean(x * x, axis=-1, keepdims=True) + RMS_EPS)
    o_ref[...] = jnp.broadcast_to(inv, o_ref.shape)


def _inv_rms(x2d, *, tm=512):
    t, d = x2d.shape
    assert t % tm == 0
    return pl.pallas_call(
        _inv_rms_kernel,
        grid=(t // tm,),
        in_specs=[pl.BlockSpec((tm, d), lambda i: (i, 0))],
        out_specs=pl.BlockSpec((tm, LANES), lambda i: (i, 0)),
        out_shape=jax.ShapeDtypeStruct((t, LANES), F32),
        compiler_params=_params("arbitrary"),
        name="inv_rms",
    )(x2d)


def _s5_kernel(x_ref, inv_ref, g_ref, wb_ref, tab_ref, pw_ref, wc_ref, dskip_ref, o_ref,
               up_scr, st_scr, xs_scr, y_scr, carry_scr):
    tc = x_ref.shape[0]
    ns = wb_ref.shape[1] // 2
    n_slabs = tc // SCAN_ROWS

    def u_block(rows, cols):
        return x_ref[rows, cols] * inv_ref[rows, :] * g_ref[:, cols]

    planes = up_scr.shape[0]
    for r in range(SCAN_ROWS):
        for c in range(planes):
            up_scr[c, pl.ds(r, n_slabs, stride=SCAN_ROWS), :] = u_block(
                slice(r * n_slabs, (r + 1) * n_slabs), slice(c * LANES, (c + 1) * LANES))
    u_perm = jnp.concatenate([up_scr[c] for c in range(planes)], axis=1)
    st_scr[...] = _dot(u_perm.astype(BF16), wb_ref[...])

    @pl.when(pl.program_id(2) == 0)
    def _():
        carry_scr[...] = jnp.zeros(carry_scr.shape, F32)

    ar = tab_ref[0]
    ai = tab_ref[1]

    def slab(s, x):
        xr, xi = x
        r0 = pl.multiple_of(s * SCAN_ROWS, SCAN_ROWS)
        xr, xi = (ar * xr - ai * xi + st_scr[pl.ds(r0, SCAN_ROWS), 0:ns],
                  ar * xi + ai * xr + st_scr[pl.ds(r0, SCAN_ROWS), ns:2 * ns])
        st_scr[pl.ds(r0, SCAN_ROWS), 0:ns] = xr
        st_scr[pl.ds(r0, SCAN_ROWS), ns:2 * ns] = xi
        return xr, xi

    zero = jnp.zeros((SCAN_ROWS, ns), F32)
    yr, yi = lax.fori_loop(0, n_slabs, slab, (zero, zero), unroll=2)

    cr = carry_scr[0]
    ci = carry_scr[1]
    for step, lag in enumerate((1, 2, 4)):
        mr = tab_ref[2 + 2 * step]
        mi = tab_ref[3 + 2 * step]
        sr = pltpu.roll(yr, lag, axis=0)
        si = pltpu.roll(yi, lag, axis=0)
        yr, yi = yr + mr * sr - mi * si, yi + mr * si + mi * sr
    pr = tab_ref[8]
    pi = tab_ref[9]
    yr, yi = yr + pr * cr - pi * ci, yi + pr * ci + pi * cr
    first = lax.broadcasted_iota(jnp.int32, (SCAN_ROWS, ns), 0) == 0
    init_r = jnp.where(first, cr, pltpu.roll(yr, 1, axis=0))
    init_i = jnp.where(first, ci, pltpu.roll(yi, 1, axis=0))
    last = SCAN_ROWS - 1
    carry_scr[0] = jnp.broadcast_to(yr[last:, :], (SCAN_ROWS, ns))
    carry_scr[1] = jnp.broadcast_to(yi[last:, :], (SCAN_ROWS, ns))

    def fix(s2, _):
        xr, xi = [], []
        for k in range(2):
            s = 2 * s2 + k
            r0 = pl.multiple_of(s * SCAN_ROWS, SCAN_ROWS)
            qr = pw_ref[0, pl.ds(s, 1), :]
            qi = pw_ref[1, pl.ds(s, 1), :]
            xr.append(st_scr[pl.ds(r0, SCAN_ROWS), 0:ns] + qr * init_r - qi * init_i)
            xi.append(st_scr[pl.ds(r0, SCAN_ROWS), ns:2 * ns] + qr * init_i + qi * init_r)
        r2 = pl.multiple_of(s2 * 2 * SCAN_ROWS, 2 * SCAN_ROWS)
        xs_scr[pl.ds(r2, 2 * SCAN_ROWS), 0:ns] = jnp.concatenate(xr, axis=0).astype(BF16)
        xs_scr[pl.ds(r2, 2 * SCAN_ROWS), ns:2 * ns] = jnp.concatenate(xi, axis=0).astype(BF16)
        return 0

    lax.fori_loop(0, n_slabs // 2, fix, 0)

    half_rows = tc // 2
    y_halves = [_dot(xs_scr[k * half_rows:(k + 1) * half_rows, :], wc_ref[...]) for k in range(2)]
    for k, y_half in enumerate(y_halves):
        for c in range(planes):
            y_scr[c, k * half_rows:(k + 1) * half_rows, :] = y_half[:, c * LANES:(c + 1) * LANES]
    for r in range(SCAN_ROWS):
        rows = slice(r * n_slabs, (r + 1) * n_slabs)
        for c in range(planes):
            cols = slice(c * LANES, (c + 1) * LANES)
            y = y_scr[c, pl.ds(r, n_slabs, stride=SCAN_ROWS), :] + dskip_ref[:, cols] * u_block(rows, cols)
            o_ref[rows, cols] = jax.nn.gelu(y, approximate=True).astype(BF16)


def _s5_core(x2d, inv_rms, g, wb, tab, pw, wc, d_skip, *, bsz, seq):
    t, d = x2d.shape
    nblk, cb, ns2 = wb.shape
    n_slabs = pw.shape[2]
    tc = n_slabs * SCAN_ROWS
    assert seq % tc == 0 and n_slabs % 2 == 0 and nblk * cb == d
    nt = seq // tc
    return pl.pallas_call(
        _s5_kernel,
        grid=(bsz, nblk, nt),
        in_specs=[
            pl.BlockSpec((tc, cb), lambda b, gb, tt: (b * nt + tt, gb)),
            pl.BlockSpec((tc, LANES), lambda b, gb, tt: (b * nt + tt, 0)),
            pl.BlockSpec((1, cb), lambda b, gb, tt: (0, gb)),
            pl.BlockSpec((None, cb, ns2), lambda b, gb, tt: (gb, 0, 0)),
            pl.BlockSpec((None, 10, SCAN_ROWS, ns2 // 2), lambda b, gb, tt: (gb, 0, 0, 0)),
            pl.BlockSpec((None, 2, n_slabs, ns2 // 2), lambda b, gb, tt: (gb, 0, 0, 0)),
            pl.BlockSpec((None, ns2, cb), lambda b, gb, tt: (gb, 0, 0)),
            pl.BlockSpec((1, cb), lambda b, gb, tt: (0, gb)),
        ],
        out_specs=pl.BlockSpec((tc, cb), lambda b, gb, tt: (b * nt + tt, gb)),
        out_shape=jax.ShapeDtypeStruct((t, d), BF16),
        scratch_shapes=[
            pltpu.VMEM((cb // LANES, tc, LANES), F32),
            pltpu.VMEM((tc, ns2), F32),
            pltpu.VMEM((tc, ns2), BF16),
            pltpu.VMEM((cb // LANES, tc, LANES), F32),
            pltpu.VMEM((2, SCAN_ROWS, ns2 // 2), F32),
        ],
        compiler_params=_params("arbitrary", "arbitrary", "arbitrary"),
        name="s5_scan",
    )(x2d, inv_rms, g.reshape(1, d), wb, tab, pw, wc, d_skip.reshape(1, d))


def _glu_res_kernel(a_ref, wv_ref, wg_ref, bv_ref, bg_ref, r_ref, o_ref, *, sub):
    a = a_ref[...]
    tn = wv_ref.shape[1]
    pairs = []
    for c in range(tn // sub):
        cols = slice(c * sub, (c + 1) * sub)
        pairs.append((cols, _dot(a, wv_ref[:, cols]), _dot(a, wg_ref[:, cols])))
    for cols, val, gate in pairs:
        o_ref[:, cols] = r_ref[:, cols] + (val + bv_ref[:, cols]) * jax.nn.sigmoid(gate + bg_ref[:, cols])


def _glu_residual(a2d, w, b, res2d, *, tm=1024, tn=1024, sub=256):
    t, k = a2d.shape
    n = w.shape[1] // 2
    assert t % tm == 0 and n % tn == 0 and tn % sub == 0
    nj = n // tn
    return pl.pallas_call(
        functools.partial(_glu_res_kernel, sub=sub),
        grid=(t // tm, nj),
        in_specs=[
            pl.BlockSpec((tm, k), lambda i, j: (i, 0)),
            pl.BlockSpec((k, tn), lambda i, j: (0, j)),
            pl.BlockSpec((k, tn), lambda i, j: (0, nj + j)),
            pl.BlockSpec((1, tn), lambda i, j: (0, j)),
            pl.BlockSpec((1, tn), lambda i, j: (0, nj + j)),
            pl.BlockSpec((tm, tn), lambda i, j: (i, j)),
        ],
        out_specs=pl.BlockSpec((tm, tn), lambda i, j: (i, j)),
        out_shape=jax.ShapeDtypeStruct((t, n), F32),
        compiler_params=_params("arbitrary", "arbitrary"),
        name="glu_residual",
    )(a2d, w, w, b.reshape(1, -1), b.reshape(1, -1), res2d)


def _s5_tables(lam_re, lam_im, log_step, b_re, b_im, c_re, c_im, *, n_slabs):
    groups, p = lam_re.shape
    hc = b_re.shape[2]
    gpb = SSM_GROUPS_PER_BLOCK
    nblk = groups // gpb
    ns = gpb * p
    step = jnp.exp(log_step)[:, None]
    log_a_re = lam_re * step
    log_a_im = lam_im * step
    mag = jnp.exp(log_a_re)
    a_re = mag * jnp.cos(log_a_im)
    a_im = mag * jnp.sin(log_a_im)
    den = lam_re * lam_re + lam_im * lam_im
    f_re = ((a_re - 1.0) * lam_re + a_im * lam_im) / den
    f_im = (a_im * lam_re - (a_re - 1.0) * lam_im) / den
    bb_re = f_re[..., None] * b_re - f_im[..., None] * b_im
    bb_im = f_re[..., None] * b_im + f_im[..., None] * b_re

    same_group = (jnp.arange(gpb * hc)[:, None] // hc) == (jnp.arange(ns)[None, :] // p)

    def group_transpose(m):
        a = m.shape[1]
        eye = jnp.broadcast_to(jnp.eye(m.shape[2], dtype=F32), (nblk, gpb) + (m.shape[2],) * 2)
        return jnp.einsum('bgkq,bgaq->bgka', eye, m.reshape(nblk, gpb, a, m.shape[2]))

    def lane_tile(m):
        copies = jnp.tile(jnp.eye(m.shape[2], dtype=F32), (1, gpb))
        return jnp.einsum('brw,wn->brn', m, copies)

    def block_diag_in(bb):
        rows = group_transpose(bb).reshape(nblk, gpb * hc, p)
        return jnp.where(same_group, lane_tile(rows), 0.0)

    def block_diag_out(c):
        cols = group_transpose(c).reshape(nblk, ns, hc)
        return jnp.where(same_group.T, lane_tile(cols), 0.0)

    wb = jnp.concatenate([block_diag_in(bb_re), block_diag_in(bb_im)], axis=2).astype(BF16)
    wc = jnp.concatenate([block_diag_out(c_re), -block_diag_out(c_im)], axis=1).astype(BF16)

    def powers(exponents):
        k = exponents.astype(F32)[None, :, None]
        k_re = k * log_a_re.reshape(nblk, 1, ns)
        k_im = k * log_a_im.reshape(nblk, 1, ns)
        return jnp.exp(k_re) * jnp.cos(k_im), jnp.exp(k_re) * jnp.sin(k_im)

    fine = 2 * SCAN_ROWS
    assert n_slabs % fine == 0
    lo_re, lo_im = powers(jnp.arange(1, fine + 1))
    hi_re, hi_im = powers(fine * jnp.arange(n_slabs // fine))
    hi_re, hi_im, lo_re, lo_im = (hi_re[:, :, None], hi_im[:, :, None], lo_re[:, None], lo_im[:, None])
    pw = jnp.stack([(hi_re * lo_re - hi_im * lo_im).reshape(nblk, n_slabs, ns),
                    (hi_re * lo_im + hi_im * lo_re).reshape(nblk, n_slabs, ns)], axis=1)
    lr, li = powers(n_slabs * jnp.arange(1, SCAN_ROWS + 1))
    rows = jnp.arange(SCAN_ROWS)[None, :, None]
    ones = jnp.ones((1, SCAN_ROWS, 1), F32)
    tabs = [a_re.reshape(nblk, 1, ns) * ones, a_im.reshape(nblk, 1, ns) * ones]
    for lag in (1, 2, 4):
        for z in (lr[:, lag - 1], li[:, lag - 1]):
            tabs.append(jnp.where(rows >= lag, z[:, None, :], 0.0))
    tabs += [lr, li]
    tab = jnp.stack(tabs, axis=1)
    return wb, tab, pw, wc


def kernel(x, norm_mix_g, norm_ffn_g, pool_w, pool_b, pool_scale, sb_w_qkv, sb_q_gain, sb_k_gain, sb_w_o, ssm_lam_re, ssm_lam_im, ssm_log_step, ssm_b_re, ssm_b_im, ssm_c_re, ssm_c_im, ssm_d, ssm_w_glu, ssm_b_glu, ffn_w_up, ffn_conv_w, ffn_conv_b, ffn_w_down):
    bsz, seq, d = x.shape
    depth = norm_mix_g.shape[0]
    heads = d // SB_HEAD_DIM
    x2d = x.reshape(bsz * seq, d)
    ffn_w_up = ffn_w_up.astype(BF16)
    ffn_w_down = ffn_w_down.astype(BF16)
    ffn_conv_b = ffn_conv_b.reshape(depth, 1, -1)
    for i in range(depth):
        kind = i % 3
        j = i // 3
        if kind == 0:
            x2d = _pool_mixer(x2d, norm_mix_g[i], pool_w[j].astype(BF16), pool_b[j], pool_scale[j],
                              seq=seq)
        elif kind == 1:
            gain_row = jnp.concatenate([
                jnp.tile(sb_q_gain[j] * (math.log2(math.e) / math.sqrt(SB_HEAD_DIM)), heads),
                jnp.tile(sb_k_gain[j], heads),
                jnp.ones((d,), F32)]).reshape(1, 3 * d)
            qkv = _qkv_proj(x2d, norm_mix_g[i], sb_w_qkv[j].astype(BF16), gain_row)
            o = _sb_attention(qkv.reshape(bsz, seq, 3 * d))
            x2d = _proj_residual(o.reshape(bsz * seq, d), sb_w_o[j].astype(BF16), x2d)
        else:
            wb, tab, pw, wc = _s5_tables(ssm_lam_re[j], ssm_lam_im[j], ssm_log_step[j],
                                         ssm_b_re[j], ssm_b_im[j], ssm_c_re[j], ssm_c_im[j],
                                         n_slabs=S5_TIME_CHUNK // SCAN_ROWS)
            y = _s5_core(x2d, _inv_rms(x2d), norm_mix_g[i], wb, tab, pw, wc, ssm_d[j],
                         bsz=bsz, seq=seq)
            x2d = _glu_residual(y, ssm_w_glu[j].astype(BF16), ssm_b_glu[j], x2d)
        x2d = _conv_ffn(x2d, norm_ffn_g[i], ffn_w_up, ffn_conv_w, ffn_conv_b, ffn_w_down,
                        layer=i, seq=seq)
    return x2d.reshape(bsz, seq, d)
```

```python
import functools
import math

import jax
import jax.numpy as jnp
from jax import lax
from jax.experimental import pallas as pl
from jax.experimental.pallas import tpu as pltpu

RMS_EPS = 1e-6
POOL_WINDOWS = (2, 4, 8, 16)
POOL_HALO = 32
SB_HEAD_DIM = 128
SSM_GROUPS_PER_BLOCK = 16
SCAN_ROWS = 8
S5_TIME_CHUNK = 1024
LANES = 128
CONV_WIDTH = 3

V7X_VMEM_LIMIT_BYTES = 60 * 1024 * 1024

F32 = jnp.float32
BF16 = jnp.bfloat16


def _params(*semantics):
    return pltpu.CompilerParams(dimension_semantics=semantics,
                                vmem_limit_bytes=V7X_VMEM_LIMIT_BYTES)


def _rms(x, g):
    ms = jnp.mean(x * x, axis=-1, keepdims=True)
    return x * lax.rsqrt(ms + RMS_EPS) * g


def _dot(a, b):
    return jnp.dot(a, b, preferred_element_type=F32)


def _ffn_kernel(x_ref, g_ref, wv_ref, wg_ref, cwv_ref, cwg_ref, cbv_ref, cbg_ref, wd_ref,
                o_ref, h_scr, act_a, act_b, tail_v, tail_g, *, tiles_per_seq):
    i = pl.program_id(0)
    j = pl.program_id(1)
    n_tiles = pl.num_programs(0) - 1
    nf = pl.num_programs(1)
    tm = x_ref.shape[0]
    fc = wv_ref.shape[1]
    step = i * nf + j

    @pl.when(step == 0)
    def _():
        act_b[...] = jnp.zeros(act_b.shape, BF16)
        o_ref[...] = jnp.zeros(o_ref.shape, F32)

    @pl.when((j == 0) & (i < n_tiles))
    def _():
        h_scr[...] = _rms(x_ref[...], g_ref[...]).astype(BF16)

        @pl.when(i % tiles_per_seq == 0)
        def _():
            tail_v[...] = jnp.zeros(tail_v.shape, F32)
            tail_g[...] = jnp.zeros(tail_g.shape, F32)

    @pl.when((j == 1) & (i < n_tiles))
    def _():
        o_ref[...] = x_ref[...]

    chunk = jnp.where(i < n_tiles, j, nf - 1)
    active = (i < n_tiles) | (j == 0)
    row = lax.broadcasted_iota(jnp.int32, (SCAN_ROWS, fc), 0)

    def causal_conv(up, tail_ref, cw_ref, cb_ref):
        prev = tail_ref[chunk]
        tail_ref[chunk] = up[tm - SCAN_ROWS:, :]
        out = cb_ref[...] + cw_ref[2:3, :] * up
        for lag in (1, 2):
            shifted = pltpu.roll(up, lag, axis=0)
            head = jnp.where(row < lag, pltpu.roll(prev, lag, axis=0), shifted[:SCAN_ROWS])
            shifted = jnp.concatenate([head, shifted[SCAN_ROWS:]], axis=0)
            out = out + cw_ref[2 - lag:3 - lag, :] * shifted
        return out

    def stages(act_new, act_old):
        h = h_scr[...]
        val = causal_conv(_dot(h, wv_ref[...]), tail_v, cwv_ref, cbv_ref)
        up_gate = _dot(h, wg_ref[...])
        o_ref[...] += _dot(act_old[...], wd_ref[...])
        gate = causal_conv(up_gate, tail_g, cwg_ref, cbg_ref)
        half = 0.5 * gate
        act_new[...] = ((half + half * jnp.tanh(half)) * val).astype(BF16)

    pl.when(active & (step % 2 == 0))(lambda: stages(act_a, act_b))
    pl.when(active & (step % 2 == 1))(lambda: stages(act_b, act_a))


def _conv_ffn(x2d, g, w_up, conv_w, conv_b, w_down, *, layer, seq, tm=1024, fc=512):
    t, d = x2d.shape
    f = w_down.shape[1]
    nf = f // fc
    n_tiles = t // tm
    assert t % tm == 0 and seq % tm == 0 and f % fc == 0
    kernel = functools.partial(_ffn_kernel, tiles_per_seq=seq // tm)

    def up_chunk(i, j):
        return jnp.where(i < n_tiles, j, nf - 1)

    def down_chunk(i, j):
        return jnp.where((i < n_tiles) & (j > 0), j - 1, nf - 1)

    def down_tile(i, j):
        return jnp.clip(jnp.where(j == 0, i - 1, i), 0, n_tiles - 1)

    return pl.pallas_call(
        kernel,
        grid=(n_tiles + 1, nf),
        in_specs=[
            pl.BlockSpec((tm, d), lambda i, j: (jnp.minimum(i, n_tiles - 1), 0)),
            pl.BlockSpec((1, d), lambda i, j: (0, 0)),
            pl.BlockSpec((None, d, fc), lambda i, j: (layer, 0, up_chunk(i, j))),
            pl.BlockSpec((None, d, fc), lambda i, j: (layer, 0, nf + up_chunk(i, j))),
            pl.BlockSpec((None, CONV_WIDTH, fc), lambda i, j: (layer, 0, up_chunk(i, j))),
            pl.BlockSpec((None, CONV_WIDTH, fc), lambda i, j: (layer, 0, nf + up_chunk(i, j))),
            pl.BlockSpec((None, 1, fc), lambda i, j: (layer, 0, up_chunk(i, j))),
            pl.BlockSpec((None, 1, fc), lambda i, j: (layer, 0, nf + up_chunk(i, j))),
            pl.BlockSpec((None, fc, d), lambda i, j: (layer, down_chunk(i, j), 0)),
        ],
        out_specs=pl.BlockSpec((tm, d), lambda i, j: (down_tile(i, j), 0)),
        out_shape=jax.ShapeDtypeStruct((t, d), F32),
        scratch_shapes=[
            pltpu.VMEM((tm, d), BF16),
            pltpu.VMEM((tm, fc), BF16),
            pltpu.VMEM((tm, fc), BF16),
            pltpu.VMEM((nf, SCAN_ROWS, fc), F32),
            pltpu.VMEM((nf, SCAN_ROWS, fc), F32),
        ],
        compiler_params=_params("arbitrary", "arbitrary"),
        name="conv_ffn",
    )(x2d, g.reshape(1, d), w_up, w_up, conv_w, conv_w, conv_b, conv_b, w_down)


def _pool_kernel(x_ref, halo_ref, g_ref, w_ref, b_ref, s_ref, o_ref, hbuf, sum_a, sum_b,
                 *, tiles_per_seq):
    i = pl.program_id(0)
    tm, d = x_ref.shape
    dg = d // len(POOL_WINDOWS)
    x = x_ref[...]
    g = g_ref[...]
    h = _rms(x, g)
    tile_in_seq = i % tiles_per_seq
    halo = jnp.where(tile_in_seq == 0, 0.0, _rms(halo_ref[...], g))
    hbuf[0:POOL_HALO, :] = halo
    hbuf[POOL_HALO:, :] = h
    head_rows = 2 * SCAN_ROWS
    assert max(POOL_WINDOWS) <= head_rows and all(w & (w - 1) == 0 for w in POOL_WINDOWS)
    t_in_seq = tile_in_seq * tm + lax.broadcasted_iota(jnp.int32, (head_rows, dg), 0)
    for gi, win in enumerate(POOL_WINDOWS):
        cols = slice(gi * dg, (gi + 1) * dg)
        levels = win.bit_length() - 1
        src, src_cols = hbuf, cols
        for k in range(1, levels + 1):
            lag = 1 << (k - 1)
            start = POOL_HALO - SCAN_ROWS * (levels - k)
            rows = POOL_HALO + tm - start
            acc = src[start:start + rows, src_cols] + src[start - lag:start - lag + rows, src_cols]
            if k < levels:
                src, src_cols = (sum_a, sum_b)[k % 2], slice(0, dg)
                src[start:start + rows, :] = acc
        hg = h[:, cols]
        head_cnt = jnp.minimum(t_in_seq + 1, win).astype(F32)
        pooled = jnp.concatenate([acc[:head_rows] / head_cnt - hg[:head_rows],
                                  acc[head_rows:] * (1.0 / win) - hg[head_rows:]], axis=0)
        y = _dot(pooled.astype(BF16), w_ref[gi])
        o_ref[:, cols] = x[:, cols] + (y + b_ref[:, cols]) * s_ref[:, cols]


def _pool_mixer(x2d, g, w, b, scale, *, seq, tm=512):
    t, d = x2d.shape
    ng, dg, _ = w.shape
    assert t % tm == 0 and seq % tm == 0 and tm % POOL_HALO == 0
    halo_blocks_per_tile = tm // POOL_HALO
    kernel = functools.partial(_pool_kernel, tiles_per_seq=seq // tm)
    return pl.pallas_call(
        kernel,
        grid=(t // tm,),
        in_specs=[
            pl.BlockSpec((tm, d), lambda i: (i, 0)),
            pl.BlockSpec((POOL_HALO, d), lambda i: (jnp.maximum(i * halo_blocks_per_tile - 1, 0), 0)),
            pl.BlockSpec((1, d), lambda i: (0, 0)),
            pl.BlockSpec((ng, dg, dg), lambda i: (0, 0, 0)),
            pl.BlockSpec((1, d), lambda i: (0, 0)),
            pl.BlockSpec((1, d), lambda i: (0, 0)),
        ],
        out_specs=pl.BlockSpec((tm, d), lambda i: (i, 0)),
        out_shape=jax.ShapeDtypeStruct((t, d), F32),
        scratch_shapes=[pltpu.VMEM((POOL_HALO + tm, d), F32),
                        pltpu.VMEM((POOL_HALO + tm, dg), F32),
                        pltpu.VMEM((POOL_HALO + tm, dg), F32)],
        compiler_params=_params("arbitrary"),
        name="pool_mixer",
    )(x2d, x2d, g.reshape(1, d), w, b.reshape(1, d), scale.reshape(1, d))


def _qkv_kernel(x_ref, g_ref, w_ref, gain_ref, o_ref, h_scr, *, norm_tiles, sub):
    j = pl.program_id(1)
    tn = w_ref.shape[1]

    @pl.when(j == 0)
    def _():
        h_scr[...] = _rms(x_ref[...], g_ref[...]).astype(BF16)

    h = h_scr[...]
    normed = j < norm_tiles
    accs = [_dot(h, w_ref[:, c * sub:(c + 1) * sub]) for c in range(tn // sub)]
    for c, acc in enumerate(accs):
        for hh in range(sub // SB_HEAD_DIM):
            head = acc[:, hh * SB_HEAD_DIM:(hh + 1) * SB_HEAD_DIM]
            cols = slice(c * sub + hh * SB_HEAD_DIM, c * sub + (hh + 1) * SB_HEAD_DIM)
            o_ref[:, cols] = jnp.where(normed, _rms(head, gain_ref[:, cols]), head).astype(BF16)


def _qkv_proj(x2d, g, w_qkv, gain_row, *, tm=1024, tn=2048, sub=512):
    t, d = x2d.shape
    n = w_qkv.shape[1]
    assert t % tm == 0 and n % tn == 0 and (2 * d) % tn == 0 and sub % SB_HEAD_DIM == 0
    assert tn % sub == 0
    kernel = functools.partial(_qkv_kernel, norm_tiles=2 * d // tn, sub=sub)
    return pl.pallas_call(
        kernel,
        grid=(t // tm, n // tn),
        in_specs=[
            pl.BlockSpec((tm, d), lambda i, j: (i, 0)),
            pl.BlockSpec((1, d), lambda i, j: (0, 0)),
            pl.BlockSpec((d, tn), lambda i, j: (0, j)),
            pl.BlockSpec((1, tn), lambda i, j: (0, j)),
        ],
        out_specs=pl.BlockSpec((tm, tn), lambda i, j: (i, j)),
        out_shape=jax.ShapeDtypeStruct((t, n), BF16),
        scratch_shapes=[pltpu.VMEM((tm, d), BF16)],
        compiler_params=_params("arbitrary", "arbitrary"),
        name="sb_qkv",
    )(x2d, g.reshape(1, d), w_qkv, gain_row)


def _sb_attn_kernel(q_ref, k_ref, v_ref, o_ref, *, chain_rows):
    qi = pl.program_id(2)
    blk = q_ref.shape[0]
    dh = SB_HEAD_DIM
    n_heads = q_ref.shape[1] // dh
    row = lax.broadcasted_iota(jnp.int32, (blk, blk), 0)
    col = lax.broadcasted_iota(jnp.int32, (blk, blk), 1)
    later = jnp.where(row > col, 1.0, 0.0).astype(BF16)
    later2 = jnp.concatenate([later, later], axis=0)
    row_g = lax.broadcasted_iota(jnp.int32, (chain_rows, blk), 0)
    col_g = lax.broadcasted_iota(jnp.int32, (chain_rows, blk), 1)

    chains = [(hh, r0) for hh in range(n_heads) for r0 in range(0, blk, chain_rows)]

    def key_block(kb, state, diagonal):
        start = pl.multiple_of(kb * blk, blk)
        cols = [slice(hh * dh, (hh + 1) * dh) for hh, _ in chains]
        rows = [slice(r0, r0 + chain_rows) for _, r0 in chains]
        zs = [lax.dot_general(q_ref[r, c], k_ref[pl.ds(start, blk), c], (((1,), (1,)), ((), ())),
                              preferred_element_type=F32) for r, c in zip(rows, cols)]
        log_betas, log_1m_betas, splits = [], [], []
        for (_, r0), z in zip(chains, zs):
            log_beta = jnp.minimum(z, 0.0) - jnp.log2(1.0 + jnp.exp2(-jnp.abs(z)))
            log_1m_beta = log_beta - z
            if diagonal:
                log_1m_beta = jnp.where(col_g < row_g + r0, log_1m_beta, 0.0)
            hi = log_1m_beta.astype(BF16)
            lo = (log_1m_beta - hi.astype(F32)).astype(BF16)
            log_betas.append(log_beta)
            log_1m_betas.append(log_1m_beta)
            splits.append(jnp.concatenate([hi, lo], axis=1))
        remains = [_dot(s, later2) for s in splits]
        ws = []
        for (_, r0), log_beta, remain, (carry, _) in zip(chains, log_betas, remains, state):
            w = jnp.exp2(log_beta + remain + carry)
            if diagonal:
                w = jnp.where(col_g < row_g + r0, w, 0.0)
            ws.append(w.astype(BF16))
        out = []
        for c, w, log_1m_beta, (carry, acc) in zip(cols, ws, log_1m_betas, state):
            out.append((carry + jnp.sum(log_1m_beta, axis=-1, keepdims=True),
                        acc + _dot(w, v_ref[pl.ds(start, blk), c])))
        return tuple(out)

    zero = (jnp.zeros((chain_rows, 1), F32), jnp.zeros((chain_rows, dh), F32))
    state = key_block(qi, (zero,) * len(chains), True)
    state = lax.fori_loop(0, qi, lambda n, st: key_block(qi - 1 - n, st, False), state)
    for (hh, r0), (_, acc) in zip(chains, state):
        o_ref[r0:r0 + chain_rows, hh * dh:(hh + 1) * dh] = acc.astype(BF16)


def _sb_attention(qkv3d, *, blk=256, heads_per_step=8, chain_rows=256):
    bsz, seq, n3 = qkv3d.shape
    d = n3 // 3
    wide = heads_per_step * SB_HEAD_DIM
    groups = d // wide
    assert seq % blk == 0 and d % wide == 0 and blk % chain_rows == 0
    return pl.pallas_call(
        functools.partial(_sb_attn_kernel, chain_rows=chain_rows),
        grid=(bsz, groups, seq // blk),
        in_specs=[
            pl.BlockSpec((None, blk, wide), lambda b, h, qi: (b, qi, h)),
            pl.BlockSpec((None, seq, wide), lambda b, h, qi: (b, 0, groups + h)),
            pl.BlockSpec((None, seq, wide), lambda b, h, qi: (b, 0, 2 * groups + h)),
        ],
        out_specs=pl.BlockSpec((None, blk, wide), lambda b, h, qi: (b, qi, h)),
        out_shape=jax.ShapeDtypeStruct((bsz, seq, d), BF16),
        compiler_params=_params("arbitrary", "arbitrary", "arbitrary"),
        name="sb_attention",
    )(qkv3d, qkv3d, qkv3d)


def _proj_res_kernel(a_ref, w_ref, r_ref, o_ref, *, sub):
    a = a_ref[...]
    tn = w_ref.shape[1]
    accs = [_dot(a, w_ref[:, c * sub:(c + 1) * sub]) for c in range(tn // sub)]
    for c, acc in enumerate(accs):
        cols = slice(c * sub, (c + 1) * sub)
        o_ref[:, cols] = r_ref[:, cols] + acc


def _proj_residual(a2d, w, res2d, *, tm=1024, tn=2048, sub=512):
    t, k = a2d.shape
    n = w.shape[1]
    assert t % tm == 0 and n % tn == 0 and tn % sub == 0
    return pl.pallas_call(
        functools.partial(_proj_res_kernel, sub=sub),
        grid=(t // tm, n // tn),
        in_specs=[
            pl.BlockSpec((tm, k), lambda i, j: (i, 0)),
            pl.BlockSpec((k, tn), lambda i, j: (0, j)),
            pl.BlockSpec((tm, tn), lambda i, j: (i, j)),
        ],
        out_specs=pl.BlockSpec((tm, tn), lambda i, j: (i, j)),
        out_shape=jax.ShapeDtypeStruct((t, n), F32),
        compiler_params=_params("arbitrary", "arbitrary"),
        name="proj_residual",
    )(a2d, w, res2d)


def _inv_rms_kernel(x_ref, o_ref):
    x = x_ref[...]
    inv = lax.rsqrt(jnp.mean(x * x, axis=-1, keepdims=True) + RMS_EPS)
    o_ref[...] = jnp.broadcast_to(inv, o_ref.shape)


def _inv_rms(x2d, *, tm=512):
    t, d = x2d.shape
    assert t % tm == 0
    return pl.pallas_call(
        _inv_rms_kernel,
        grid=(t // tm,),
        in_specs=[pl.BlockSpec((tm, d), lambda i: (i, 0))],
        out_specs=pl.BlockSpec((tm, LANES), lambda i: (i, 0)),
        out_shape=jax.ShapeDtypeStruct((t, LANES), F32),
        compiler_params=_params("arbitrary"),
        name="inv_rms",
    )(x2d)


def _s5_kernel(x_ref, inv_ref, g_ref, wb_ref, tab_ref, pw_ref, wc_ref, dskip_ref, o_ref,
               up_scr, st_scr, xs_scr, y_scr, carry_scr):
    tc = x_ref.shape[0]
    ns = wb_ref.shape[1] // 2
    n_slabs = tc // SCAN_ROWS

    def u_block(rows, cols):
        return x_ref[rows, cols] * inv_ref[rows, :] * g_ref[:, cols]

    planes = up_scr.shape[0]
    for r in range(SCAN_ROWS):
        for c in range(planes):
            up_scr[c, pl.ds(r, n_slabs, stride=SCAN_ROWS), :] = u_block(
                slice(r * n_slabs, (r + 1) * n_slabs), slice(c * LANES, (c + 1) * LANES))
    u_perm = jnp.concatenate([up_scr[c] for c in range(planes)], axis=1)
    st_scr[...] = _dot(u_perm.astype(BF16), wb_ref[...])

    @pl.when(pl.program_id(2) == 0)
    def _():
        carry_scr[...] = jnp.zeros(carry_scr.shape, F32)

    ar = tab_ref[0]
    ai = tab_ref[1]

    def slab(s, x):
        xr, xi = x
        r0 = pl.multiple_of(s * SCAN_ROWS, SCAN_ROWS)
        xr, xi = (ar * xr - ai * xi + st_scr[pl.ds(r0, SCAN_ROWS), 0:ns],
                  ar * xi + ai * xr + st_scr[pl.ds(r0, SCAN_ROWS), ns:2 * ns])
        st_scr[pl.ds(r0, SCAN_ROWS), 0:ns] = xr
        st_scr[pl.ds(r0, SCAN_ROWS), ns:2 * ns] = xi
        return xr, xi

    zero = jnp.zeros((SCAN_ROWS, ns), F32)
    yr, yi = lax.fori_loop(0, n_slabs, slab, (zero, zero), unroll=2)

    cr = carry_scr[0]
    ci = carry_scr[1]
    for step, lag in enumerate((1, 2, 4)):
        mr = tab_ref[2 + 2 * step]
        mi = tab_ref[3 + 2 * step]
        sr = pltpu.roll(yr, lag, axis=0)
        si = pltpu.roll(yi, lag, axis=0)
        yr, yi = yr + mr * sr - mi * si, yi + mr * si + mi * sr
    pr = tab_ref[8]
    pi = tab_ref[9]
    yr, yi = yr + pr * cr - pi * ci, yi + pr * ci + pi * cr
    first = lax.broadcasted_iota(jnp.int32, (SCAN_ROWS, ns), 0) == 0
    init_r = jnp.where(first, cr, pltpu.roll(yr, 1, axis=0))
    init_i = jnp.where(first, ci, pltpu.roll(yi, 1, axis=0))
    last = SCAN_ROWS - 1
    carry_scr[0] = jnp.broadcast_to(yr[last:, :], (SCAN_ROWS, ns))
    carry_scr[1] = jnp.broadcast_to(yi[last:, :], (SCAN_ROWS, ns))

    def fix(s2, _):
        xr, xi = [], []
        for k in range(2):
            s = 2 * s2 + k
            r0 = pl.multiple_of(s * SCAN_ROWS, SCAN_ROWS)
            qr = pw_ref[0, pl.ds(s, 1), :]
            qi = pw_ref[1, pl.ds(s, 1), :]
            xr.append(st_scr[pl.ds(r0, SCAN_ROWS), 0:ns] + qr * init_r - qi * init_i)
            xi.append(st_scr[pl.ds(r0, SCAN_ROWS), ns:2 * ns] + qr * init_i + qi * init_r)
        r2 = pl.multiple_of(s2 * 2 * SCAN_ROWS, 2 * SCAN_ROWS)
        xs_scr[pl.ds(r2, 2 * SCAN_ROWS), 0:ns] = jnp.concatenate(xr, axis=0).astype(BF16)
        xs_scr[pl.ds(r2, 2 * SCAN_ROWS), ns:2 * ns] = jnp.concatenate(xi, axis=0).astype(BF16)
        return 0

    lax.fori_loop(0, n_slabs // 2, fix, 0)

    half_rows = tc // 2
    y_halves = [_dot(xs_scr[k * half_rows:(k + 1) * half_rows, :], wc_ref[...]) for k in range(2)]
    for k, y_half in enumerate(y_halves):
        for c in range(planes):
            y_scr[c, k * half_rows:(k + 1) * half_rows, :] = y_half[:, c * LANES:(c + 1) * LANES]
    for r in range(SCAN_ROWS):
        rows = slice(r * n_slabs, (r + 1) * n_slabs)
        for c in range(planes):
            cols = slice(c * LANES, (c + 1) * LANES)
            y = y_scr[c, pl.ds(r, n_slabs, stride=SCAN_ROWS), :] + dskip_ref[:, cols] * u_block(rows, cols)
            o_ref[rows, cols] = jax.nn.gelu(y, approximate=True).astype(BF16)


def _s5_core(x2d, inv_rms, g, wb, tab, pw, wc, d_skip, *, bsz, seq):
    t, d = x2d.shape
    nblk, cb, ns2 = wb.shape
    n_slabs = pw.shape[2]
    tc = n_slabs * SCAN_ROWS
    assert seq % tc == 0 and n_slabs % 2 == 0 and nblk * cb == d
    nt = seq // tc
    return pl.pallas_call(
        _s5_kernel,
        grid=(bsz, nblk, nt),
        in_specs=[
            pl.BlockSpec((tc, cb), lambda b, gb, tt: (b * nt + tt, gb)),
            pl.BlockSpec((tc, LANES), lambda b, gb, tt: (b * nt + tt, 0)),
            pl.BlockSpec((1, cb), lambda b, gb, tt: (0, gb)),
            pl.BlockSpec((None, cb, ns2), lambda b, gb, tt: (gb, 0, 0)),
            pl.BlockSpec((None, 10, SCAN_ROWS, ns2 // 2), lambda b, gb, tt: (gb, 0, 0, 0)),
            pl.BlockSpec((None, 2, n_slabs, ns2 // 2), lambda b, gb, tt: (gb, 0, 0, 0)),
            pl.BlockSpec((None, ns2, cb), lambda b, gb, tt: (gb, 0, 0)),
            pl.BlockSpec((1, cb), lambda b, gb, tt: (0, gb)),
        ],
        out_specs=pl.BlockSpec((tc, cb), lambda b, gb, tt: (b * nt + tt, gb)),
        out_shape=jax.ShapeDtypeStruct((t, d), BF16),
        scratch_shapes=[
            pltpu.VMEM((cb // LANES, tc, LANES), F32),
            pltpu.VMEM((tc, ns2), F32),
            pltpu.VMEM((tc, ns2), BF16),
            pltpu.VMEM((cb // LANES, tc, LANES), F32),
            pltpu.VMEM((2, SCAN_ROWS, ns2 // 2), F32),
        ],
        compiler_params=_params("arbitrary", "arbitrary", "arbitrary"),
        name="s5_scan",
    )(x2d, inv_rms, g.reshape(1, d), wb, tab, pw, wc, d_skip.reshape(1, d))


def _glu_res_kernel(a_ref, wv_ref, wg_ref, bv_ref, bg_ref, r_ref, o_ref, *, sub):
    a = a_ref[...]
    tn = wv_ref.shape[1]
    pairs = []
    for c in range(tn // sub):
        cols = slice(c * sub, (c + 1) * sub)
        pairs.append((cols, _dot(a, wv_ref[:, cols]), _dot(a, wg_ref[:, cols])))
    for cols, val, gate in pairs:
        o_ref[:, cols] = r_ref[:, cols] + (val + bv_ref[:, cols]) * jax.nn.sigmoid(gate + bg_ref[:, cols])


def _glu_residual(a2d, w, b, res2d, *, tm=1024, tn=1024, sub=256):
    t, k = a2d.shape
    n = w.shape[1] // 2
    assert t % tm == 0 and n % tn == 0 and tn % sub == 0
    nj = n // tn
    return pl.pallas_call(
        functools.partial(_glu_res_kernel, sub=sub),
        grid=(t // tm, nj),
        in_specs=[
            pl.BlockSpec((tm, k), lambda i, j: (i, 0)),
            pl.BlockSpec((k, tn), lambda i, j: (0, j)),
            pl.BlockSpec((k, tn), lambda i, j: (0, nj + j)),
            pl.BlockSpec((1, tn), lambda i, j: (0, j)),
            pl.BlockSpec((1, tn), lambda i, j: (0, nj + j)),
            pl.BlockSpec((tm, tn), lambda i, j: (i, j)),
        ],
        out_specs=pl.BlockSpec((tm, tn), lambda i, j: (i, j)),
        out_shape=jax.ShapeDtypeStruct((t, n), F32),
        compiler_params=_params("arbitrary", "arbitrary"),
        name="glu_residual",
    )(a2d, w, w, b.reshape(1, -1), b.reshape(1, -1), res2d)


def _s5_tables(lam_re, lam_im, log_step, b_re, b_im, c_re, c_im, *, n_slabs):
    groups, p = lam_re.shape
    hc = b_re.shape[2]
    gpb = SSM_GROUPS_PER_BLOCK
    nblk = groups // gpb
    ns = gpb * p
    step = jnp.exp(log_step)[:, None]
    log_a_re = lam_re * step
    log_a_im = lam_im * step
    mag = jnp.exp(log_a_re)
    a_re = mag * jnp.cos(log_a_im)
    a_im = mag * jnp.sin(log_a_im)
    den = lam_re * lam_re + lam_im * lam_im
    f_re = ((a_re - 1.0) * lam_re + a_im * lam_im) / den
    f_im = (a_im * lam_re - (a_re - 1.0) * lam_im) / den
    bb_re = f_re[..., None] * b_re - f_im[..., None] * b_im
    bb_im = f_re[..., None] * b_im + f_im[..., None] * b_re

    same_group = (jnp.arange(gpb * hc)[:, None] // hc) == (jnp.arange(ns)[None, :] // p)

    def group_transpose(m):
        a = m.shape[1]
        eye = jnp.broadcast_to(jnp.eye(m.shape[2], dtype=F32), (nblk, gpb) + (m.shape[2],) * 2)
        return jnp.einsum('bgkq,bgaq->bgka', eye, m.reshape(nblk, gpb, a, m.shape[2]))

    def lane_tile(m):
        copies = jnp.tile(jnp.eye(m.shape[2], dtype=F32), (1, gpb))
        return jnp.einsum('brw,wn->brn', m, copies)

    def block_diag_in(bb):
        rows = group_transpose(bb).reshape(nblk, gpb * hc, p)
        return jnp.where(same_group, lane_tile(rows), 0.0)

    def block_diag_out(c):
        cols = group_transpose(c).reshape(nblk, ns, hc)
        return jnp.where(same_group.T, lane_tile(cols), 0.0)

    wb = jnp.concatenate([block_diag_in(bb_re), block_diag_in(bb_im)], axis=2).astype(BF16)
    wc = jnp.concatenate([block_diag_out(c_re), -block_diag_out(c_im)], axis=1).astype(BF16)

    def powers(exponents):
        k = exponents.astype(F32)[None, :, None]
        k_re = k * log_a_re.reshape(nblk, 1, ns)
        k_im = k * log_a_im.reshape(nblk, 1, ns)
        return jnp.exp(k_re) * jnp.cos(k_im), jnp.exp(k_re) * jnp.sin(k_im)

    fine = 2 * SCAN_ROWS
    assert n_slabs % fine == 0
    lo_re, lo_im = powers(jnp.arange(1, fine + 1))
    hi_re, hi_im = powers(fine * jnp.arange(n_slabs // fine))
    hi_re, hi_im, lo_re, lo_im = (hi_re[:, :, None], hi_im[:, :, None], lo_re[:, None], lo_im[:, None])
    pw = jnp.stack([(hi_re * lo_re - hi_im * lo_im).reshape(nblk, n_slabs, ns),
                    (hi_re * lo_im + hi_im * lo_re).reshape(nblk, n_slabs, ns)], axis=1)
    lr, li = powers(n_slabs * jnp.arange(1, SCAN_ROWS + 1))
    rows = jnp.arange(SCAN_ROWS)[None, :, None]
    ones = jnp.ones((1, SCAN_ROWS, 1), F32)
    tabs = [a_re.reshape(nblk, 1, ns) * ones, a_im.reshape(nblk, 1, ns) * ones]
    for lag in (1, 2, 4):
        for z in (lr[:, lag - 1], li[:, lag - 1]):
            tabs.append(jnp.where(rows >= lag, z[:, None, :], 0.0))
    tabs += [lr, li]
    tab = jnp.stack(tabs, axis=1)
    return wb, tab, pw, wc


def kernel(x, norm_mix_g, norm_ffn_g, pool_w, pool_b, pool_scale, sb_w_qkv, sb_q_gain, sb_k_gain, sb_w_o, ssm_lam_re, ssm_lam_im, ssm_log_step, ssm_b_re, ssm_b_im, ssm_c_re, ssm_c_im, ssm_d, ssm_w_glu, ssm_b_glu, ffn_w_up, ffn_conv_w, ffn_conv_b, ffn_w_down):
    bsz, seq, d = x.shape
    depth = norm_mix_g.shape[0]
    heads = d // SB_HEAD_DIM
    x2d = x.reshape(bsz * seq, d)
    ffn_w_up = ffn_w_up.astype(BF16)
    ffn_w_down = ffn_w_down.astype(BF16)
    ffn_conv_b = ffn_conv_b.reshape(depth, 1, -1)
    for i in range(depth):
        kind = i % 3
        j = i // 3
        if kind == 0:
            x2d = _pool_mixer(x2d, norm_mix_g[i], pool_w[j].astype(BF16), pool_b[j], pool_scale[j],
                              seq=seq)
        elif kind == 1:
            gain_row = jnp.concatenate([
                jnp.tile(sb_q_gain[j] * (math.log2(math.e) / math.sqrt(SB_HEAD_DIM)), heads),
                jnp.tile(sb_k_gain[j], heads),
                jnp.ones((d,), F32)]).reshape(1, 3 * d)
            qkv = _qkv_proj(x2d, norm_mix_g[i], sb_w_qkv[j].astype(BF16), gain_row)
            o = _sb_attention(qkv.reshape(bsz, seq, 3 * d))
            x2d = _proj_residual(o.reshape(bsz * seq, d), sb_w_o[j].astype(BF16), x2d)
        else:
            wb, tab, pw, wc = _s5_tables(ssm_lam_re[j], ssm_lam_im[j], ssm_log_step[j],
                                         ssm_b_re[j], ssm_b_im[j], ssm_c_re[j], ssm_c_im[j],
                                         n_slabs=S5_TIME_CHUNK // SCAN_ROWS)
            y = _s5_core(x2d, _inv_rms(x2d), norm_mix_g[i], wb, tab, pw, wc, ssm_d[j],
                         bsz=bsz, seq=seq)
            x2d = _glu_residual(y, ssm_w_glu[j].astype(BF16), ssm_b_glu[j], x2d)
        x2d = _conv_ffn(x2d, norm_ffn_g[i], ffn_w_up, ffn_conv_w, ffn_conv_b, ffn_w_down,
                        layer=i, seq=seq)
    return x2d.reshape(bsz, seq, d)
```

```python
import functools
import math

import jax
import jax.numpy as jnp
from jax import lax
from jax.experimental import pallas as pl
from jax.experimental.pallas import tpu as pltpu

RMS_EPS = 1e-6
POOL_WINDOWS = (2, 4, 8, 16)
POOL_HALO = 32
SB_HEAD_DIM = 128
SSM_GROUPS_PER_BLOCK = 16
SCAN_ROWS = 8
S5_TIME_CHUNK = 2048
LANES = 128
CONV_WIDTH = 3

V7X_VMEM_LIMIT_BYTES = 60 * 1024 * 1024

F32 = jnp.float32
BF16 = jnp.bfloat16


def _params(*semantics):
    return pltpu.CompilerParams(dimension_semantics=semantics,
                                vmem_limit_bytes=V7X_VMEM_LIMIT_BYTES)


def _rms(x, g):
    ms = jnp.mean(x * x, axis=-1, keepdims=True)
    return x * lax.rsqrt(ms + RMS_EPS) * g


def _dot(a, b):
    return jnp.dot(a, b, preferred_element_type=F32)


def _ffn_kernel(x_ref, g_ref, wv_ref, wg_ref, cwv_ref, cwg_ref, cbv_ref, cbg_ref, wd_ref,
                o_ref, h_scr, act_a, act_b, tail_v, tail_g, *, tiles_per_seq):
    i = pl.program_id(0)
    j = pl.program_id(1)
    n_tiles = pl.num_programs(0) - 1
    nf = pl.num_programs(1)
    tm = x_ref.shape[0]
    fc = wv_ref.shape[1]
    step = i * nf + j

    @pl.when(step == 0)
    def _():
        act_b[...] = jnp.zeros(act_b.shape, BF16)
        o_ref[...] = jnp.zeros(o_ref.shape, F32)

    @pl.when((j == 0) & (i < n_tiles))
    def _():
        h_scr[...] = _rms(x_ref[...], g_ref[...]).astype(BF16)

        @pl.when(i % tiles_per_seq == 0)
        def _():
            tail_v[...] = jnp.zeros(tail_v.shape, F32)
            tail_g[...] = jnp.zeros(tail_g.shape, F32)

    @pl.when((j == 1) & (i < n_tiles))
    def _():
        o_ref[...] = x_ref[...]

    chunk = jnp.where(i < n_tiles, j, nf - 1)
    active = (i < n_tiles) | (j == 0)
    row = lax.broadcasted_iota(jnp.int32, (SCAN_ROWS, fc), 0)

    def causal_conv(up, tail_ref, cw_ref, cb_ref):
        prev = tail_ref[chunk]
        tail_ref[chunk] = up[tm - SCAN_ROWS:, :]
        out = cb_ref[...] + cw_ref[2:3, :] * up
        for lag in (1, 2):
            shifted = pltpu.roll(up, lag, axis=0)
            head = jnp.where(row < lag, pltpu.roll(prev, lag, axis=0), shifted[:SCAN_ROWS])
            shifted = jnp.concatenate([head, shifted[SCAN_ROWS:]], axis=0)
            out = out + cw_ref[2 - lag:3 - lag, :] * shifted
        return out

    def stages(act_new, act_old):
        h = h_scr[...]
        val = causal_conv(_dot(h, wv_ref[...]), tail_v, cwv_ref, cbv_ref)
        up_gate = _dot(h, wg_ref[...])
        o_ref[...] += _dot(act_old[...], wd_ref[...])
        gate = causal_conv(up_gate, tail_g, cwg_ref, cbg_ref)
        half = 0.5 * gate
        act_new[...] = ((half + half * jnp.tanh(half)) * val).astype(BF16)

    pl.when(active & (step % 2 == 0))(lambda: stages(act_a, act_b))
    pl.when(active & (step % 2 == 1))(lambda: stages(act_b, act_a))


def _conv_ffn(x2d, g, w_up, conv_w, conv_b, w_down, *, layer, seq, tm=1024, fc=512):
    t, d = x2d.shape
    f = w_down.shape[1]
    nf = f // fc
    n_tiles = t // tm
    assert t % tm == 0 and seq % tm == 0 and f % fc == 0
    kernel = functools.partial(_ffn_kernel, tiles_per_seq=seq // tm)

    def up_chunk(i, j):
        return jnp.where(i < n_tiles, j, nf - 1)

    def down_chunk(i, j):
        return jnp.where((i < n_tiles) & (j > 0), j - 1, nf - 1)

    def down_tile(i, j):
        return jnp.clip(jnp.where(j == 0, i - 1, i), 0, n_tiles - 1)

    return pl.pallas_call(
        kernel,
        grid=(n_tiles + 1, nf),
        in_specs=[
            pl.BlockSpec((tm, d), lambda i, j: (jnp.minimum(i, n_tiles - 1), 0)),
            pl.BlockSpec((1, d), lambda i, j: (0, 0)),
            pl.BlockSpec((None, d, fc), lambda i, j: (layer, 0, up_chunk(i, j))),
            pl.BlockSpec((None, d, fc), lambda i, j: (layer, 0, nf + up_chunk(i, j))),
            pl.BlockSpec((None, CONV_WIDTH, fc), lambda i, j: (layer, 0, up_chunk(i, j))),
            pl.BlockSpec((None, CONV_WIDTH, fc), lambda i, j: (layer, 0, nf + up_chunk(i, j))),
            pl.BlockSpec((None, 1, fc), lambda i, j: (layer, 0, up_chunk(i, j))),
            pl.BlockSpec((None, 1, fc), lambda i, j: (layer, 0, nf + up_chunk(i, j))),
            pl.BlockSpec((None, fc, d), lambda i, j: (layer, down_chunk(i, j), 0)),
        ],
        out_specs=pl.BlockSpec((tm, d), lambda i, j: (down_tile(i, j), 0)),
        out_shape=jax.ShapeDtypeStruct((t, d), F32),
        scratch_shapes=[
            pltpu.VMEM((tm, d), BF16),
            pltpu.VMEM((tm, fc), BF16),
            pltpu.VMEM((tm, fc), BF16),
            pltpu.VMEM((nf, SCAN_ROWS, fc), F32),
            pltpu.VMEM((nf, SCAN_ROWS, fc), F32),
        ],
        compiler_params=_params("arbitrary", "arbitrary"),
        name="conv_ffn",
    )(x2d, g.reshape(1, d), w_up, w_up, conv_w, conv_w, conv_b, conv_b, w_down)


def _pool_kernel(x_ref, halo_ref, g_ref, w_ref, b_ref, s_ref, o_ref, hbuf, sum_a, sum_b,
                 *, tiles_per_seq):
    i = pl.program_id(0)
    tm, d = x_ref.shape
    dg = d // len(POOL_WINDOWS)
    x = x_ref[...]
    g = g_ref[...]
    h = _rms(x, g)
    tile_in_seq = i % tiles_per_seq
    halo = jnp.where(tile_in_seq == 0, 0.0, _rms(halo_ref[...], g))
    hbuf[0:POOL_HALO, :] = halo
    hbuf[POOL_HALO:, :] = h
    head_rows = 2 * SCAN_ROWS
    assert max(POOL_WINDOWS) <= head_rows and all(w & (w - 1) == 0 for w in POOL_WINDOWS)
    t_in_seq = tile_in_seq * tm + lax.broadcasted_iota(jnp.int32, (head_rows, dg), 0)
    for gi, win in enumerate(POOL_WINDOWS):
        cols = slice(gi * dg, (gi + 1) * dg)
        levels = win.bit_length() - 1
        src, src_cols = hbuf, cols
        for k in range(1, levels + 1):
            lag = 1 << (k - 1)
            start = POOL_HALO - SCAN_ROWS * (levels - k)
            rows = POOL_HALO + tm - start
            acc = src[start:start + rows, src_cols] + src[start - lag:start - lag + rows, src_cols]
            if k < levels:
                src, src_cols = (sum_a, sum_b)[k % 2], slice(0, dg)
                src[start:start + rows, :] = acc
        hg = h[:, cols]
        head_cnt = jnp.minimum(t_in_seq + 1, win).astype(F32)
        pooled = jnp.concatenate([acc[:head_rows] / head_cnt - hg[:head_rows],
                                  acc[head_rows:] * (1.0 / win) - hg[head_rows:]], axis=0)
        y = _dot(pooled.astype(BF16), w_ref[gi])
        o_ref[:, cols] = x[:, cols] + (y + b_ref[:, cols]) * s_ref[:, cols]


def _pool_mixer(x2d, g, w, b, scale, *, seq, tm=512):
    t, d = x2d.shape
    ng, dg, _ = w.shape
    assert t % tm == 0 and seq % tm == 0 and tm % POOL_HALO == 0
    halo_blocks_per_tile = tm // POOL_HALO
    kernel = functools.partial(_pool_kernel, tiles_per_seq=seq // tm)
    return pl.pallas_call(
        kernel,
        grid=(t // tm,),
        in_specs=[
            pl.BlockSpec((tm, d), lambda i: (i, 0)),
            pl.BlockSpec((POOL_HALO, d), lambda i: (jnp.maximum(i * halo_blocks_per_tile - 1, 0), 0)),
            pl.BlockSpec((1, d), lambda i: (0, 0)),
            pl.BlockSpec((ng, dg, dg), lambda i: (0, 0, 0)),
            pl.BlockSpec((1, d), lambda i: (0, 0)),
            pl.BlockSpec((1, d), lambda i: (0, 0)),
        ],
        out_specs=pl.BlockSpec((tm, d), lambda i: (i, 0)),
        out_shape=jax.ShapeDtypeStruct((t, d), F32),
        scratch_shapes=[pltpu.VMEM((POOL_HALO + tm, d), F32),
                        pltpu.VMEM((POOL_HALO + tm, dg), F32),
                        pltpu.VMEM((POOL_HALO + tm, dg), F32)],
        compiler_params=_params("arbitrary"),
        name="pool_mixer",
    )(x2d, x2d, g.reshape(1, d), w, b.reshape(1, d), scale.reshape(1, d))


def _qkv_kernel(x_ref, g_ref, w_ref, gain_ref, o_ref, h_scr, *, norm_tiles, sub):
    j = pl.program_id(1)
    tn = w_ref.shape[1]

    @pl.when(j == 0)
    def _():
        h_scr[...] = _rms(x_ref[...], g_ref[...]).astype(BF16)

    h = h_scr[...]
    normed = j < norm_tiles
    accs = [_dot(h, w_ref[:, c * sub:(c + 1) * sub]) for c in range(tn // sub)]
    for c, acc in enumerate(accs):
        for hh in range(sub // SB_HEAD_DIM):
            head = acc[:, hh * SB_HEAD_DIM:(hh + 1) * SB_HEAD_DIM]
            cols = slice(c * sub + hh * SB_HEAD_DIM, c * sub + (hh + 1) * SB_HEAD_DIM)
            o_ref[:, cols] = jnp.where(normed, _rms(head, gain_ref[:, cols]), head).astype(BF16)


def _qkv_proj(x2d, g, w_qkv, gain_row, *, tm=1024, tn=2048, sub=512):
    t, d = x2d.shape
    n = w_qkv.shape[1]
    assert t % tm == 0 and n % tn == 0 and (2 * d) % tn == 0 and sub % SB_HEAD_DIM == 0
    assert tn % sub == 0
    kernel = functools.partial(_qkv_kernel, norm_tiles=2 * d // tn, sub=sub)
    return pl.pallas_call(
        kernel,
        grid=(t // tm, n // tn),
        in_specs=[
            pl.BlockSpec((tm, d), lambda i, j: (i, 0)),
            pl.BlockSpec((1, d), lambda i, j: (0, 0)),
            pl.BlockSpec((d, tn), lambda i, j: (0, j)),
            pl.BlockSpec((1, tn), lambda i, j: (0, j)),
        ],
        out_specs=pl.BlockSpec((tm, tn), lambda i, j: (i, j)),
        out_shape=jax.ShapeDtypeStruct((t, n), BF16),
        scratch_shapes=[pltpu.VMEM((tm, d), BF16)],
        compiler_params=_params("arbitrary", "arbitrary"),
        name="sb_qkv",
    )(x2d, g.reshape(1, d), w_qkv, gain_row)


def _sb_attn_kernel(q_ref, k_ref, v_ref, o_ref, *, chain_rows):
    qi = pl.program_id(2)
    blk = q_ref.shape[0]
    dh = SB_HEAD_DIM
    n_heads = q_ref.shape[1] // dh
    row = lax.broadcasted_iota(jnp.int32, (blk, blk), 0)
    col = lax.broadcasted_iota(jnp.int32, (blk, blk), 1)
    later = jnp.where(row > col, 1.0, 0.0).astype(BF16)
    later2 = jnp.concatenate([later, later], axis=0)
    row_g = lax.broadcasted_iota(jnp.int32, (chain_rows, blk), 0)
    col_g = lax.broadcasted_iota(jnp.int32, (chain_rows, blk), 1)

    chains = [(hh, r0) for hh in range(n_heads) for r0 in range(0, blk, chain_rows)]

    def key_block(kb, state, diagonal):
        start = pl.multiple_of(kb * blk, blk)
        cols = [slice(hh * dh, (hh + 1) * dh) for hh, _ in chains]
        rows = [slice(r0, r0 + chain_rows) for _, r0 in chains]
        zs = [lax.dot_general(q_ref[r, c], k_ref[pl.ds(start, blk), c], (((1,), (1,)), ((), ())),
                              preferred_element_type=F32) for r, c in zip(rows, cols)]
        log_betas, log_1m_betas, splits = [], [], []
        for (_, r0), z in zip(chains, zs):
            log_beta = jnp.minimum(z, 0.0) - jnp.log2(1.0 + jnp.exp2(-jnp.abs(z)))
            log_1m_beta = log_beta - z
            if diagonal:
                log_1m_beta = jnp.where(col_g < row_g + r0, log_1m_beta, 0.0)
            hi = log_1m_beta.astype(BF16)
            lo = (log_1m_beta - hi.astype(F32)).astype(BF16)
            log_betas.append(log_beta)
            log_1m_betas.append(log_1m_beta)
            splits.append(jnp.concatenate([hi, lo], axis=1))
        remains = [_dot(s, later2) for s in splits]
        ws = []
        for (_, r0), log_beta, remain, (carry, _) in zip(chains, log_betas, remains, state):
            w = jnp.exp2(log_beta + remain + carry)
            if diagonal:
                w = jnp.where(col_g < row_g + r0, w, 0.0)
            ws.append(w.astype(BF16))
        out = []
        for c, w, log_1m_beta, (carry, acc) in zip(cols, ws, log_1m_betas, state):
            out.append((carry + jnp.sum(log_1m_beta, axis=-1, keepdims=True),
                        acc + _dot(w, v_ref[pl.ds(start, blk), c])))
        return tuple(out)

    zero = (jnp.zeros((chain_rows, 1), F32), jnp.zeros((chain_rows, dh), F32))
    state = key_block(qi, (zero,) * len(chains), True)
    state = lax.fori_loop(0, qi, lambda n, st: key_block(qi - 1 - n, st, False), state)
    for (hh, r0), (_, acc) in zip(chains, state):
        o_ref[r0:r0 + chain_rows, hh * dh:(hh + 1) * dh] = acc.astype(BF16)


def _sb_attention(qkv3d, *, blk=256, heads_per_step=8, chain_rows=256):
    bsz, seq, n3 = qkv3d.shape
    d = n3 // 3
    wide = heads_per_step * SB_HEAD_DIM
    groups = d // wide
    assert seq % blk == 0 and d % wide == 0 and blk % chain_rows == 0
    return pl.pallas_call(
        functools.partial(_sb_attn_kernel, chain_rows=chain_rows),
        grid=(bsz, groups, seq // blk),
        in_specs=[
            pl.BlockSpec((None, blk, wide), lambda b, h, qi: (b, qi, h)),
            pl.BlockSpec((None, seq, wide), lambda b, h, qi: (b, 0, groups + h)),
            pl.BlockSpec((None, seq, wide), lambda b, h, qi: (b, 0, 2 * groups + h)),
        ],
        out_specs=pl.BlockSpec((None, blk, wide), lambda b, h, qi: (b, qi, h)),
        out_shape=jax.ShapeDtypeStruct((bsz, seq, d), BF16),
        compiler_params=_params("arbitrary", "arbitrary", "arbitrary"),
        name="sb_attention",
    )(qkv3d, qkv3d, qkv3d)


def _proj_res_kernel(a_ref, w_ref, r_ref, o_ref, *, sub):
    a = a_ref[...]
    tn = w_ref.shape[1]
    accs = [_dot(a, w_ref[:, c * sub:(c + 1) * sub]) for c in range(tn // sub)]
    for c, acc in enumerate(accs):
        cols = slice(c * sub, (c + 1) * sub)
        o_ref[:, cols] = r_ref[:, cols] + acc


def _proj_residual(a2d, w, res2d, *, tm=1024, tn=2048, sub=512):
    t, k = a2d.shape
    n = w.shape[1]
    assert t % tm == 0 and n % tn == 0 and tn % sub == 0
    return pl.pallas_call(
        functools.partial(_proj_res_kernel, sub=sub),
        grid=(t // tm, n // tn),
        in_specs=[
            pl.BlockSpec((tm, k), lambda i, j: (i, 0)),
            pl.BlockSpec((k, tn), lambda i, j: (0, j)),
            pl.BlockSpec((tm, tn), lambda i, j: (i, j)),
        ],
        out_specs=pl.BlockSpec((tm, tn), lambda i, j: (i, j)),
        out_shape=jax.ShapeDtypeStruct((t, n), F32),
        compiler_params=_params("arbitrary", "arbitrary"),
        name="proj_residual",
    )(a2d, w, res2d)


def _inv_rms_kernel(x_ref, o_ref):
    x = x_ref[...]
    inv = lax.rsqrt(jnp.mean(x * x, axis=-1, keepdims=True) + RMS_EPS)
    o_ref[...] = jnp.broadcast_to(inv, o_ref.shape)


def _inv_rms(x2d, *, tm=512):
    t, d = x2d.shape
    assert t % tm == 0
    return pl.pallas_call(
        _inv_rms_kernel,
        grid=(t // tm,),
        in_specs=[pl.BlockSpec((tm, d), lambda i: (i, 0))],
        out_specs=pl.BlockSpec((tm, LANES), lambda i: (i, 0)),
        out_shape=jax.ShapeDtypeStruct((t, LANES), F32),
        compiler_params=_params("arbitrary"),
        name="inv_rms",
    )(x2d)


def _s5_kernel(x_ref, inv_ref, g_ref, wb_ref, tab_ref, pw_ref, wc_ref, dskip_ref, o_ref,
               up_scr, st_scr, xs_scr, y_scr, carry_scr):
    tc = x_ref.shape[0]
    ns = wb_ref.shape[1] // 2
    n_slabs = tc // SCAN_ROWS

    def u_block(rows, cols):
        return x_ref[rows, cols] * inv_ref[rows, :] * g_ref[:, cols]

    planes = up_scr.shape[0]
    for r in range(SCAN_ROWS):
        for c in range(planes):
            up_scr[c, pl.ds(r, n_slabs, stride=SCAN_ROWS), :] = u_block(
                slice(r * n_slabs, (r + 1) * n_slabs), slice(c * LANES, (c + 1) * LANES))
    u_perm = jnp.concatenate([up_scr[c] for c in range(planes)], axis=1)
    st_scr[...] = _dot(u_perm.astype(BF16), wb_ref[...])

    @pl.when(pl.program_id(2) == 0)
    def _():
        carry_scr[...] = jnp.zeros(carry_scr.shape, F32)

    ar = tab_ref[0]
    ai = tab_ref[1]

    def slab(s, x):
        xr, xi = x
        r0 = pl.multiple_of(s * SCAN_ROWS, SCAN_ROWS)
        xr, xi = (ar * xr - ai * xi + st_scr[pl.ds(r0, SCAN_ROWS), 0:ns],
                  ar * xi + ai * xr + st_scr[pl.ds(r0, SCAN_ROWS), ns:2 * ns])
        st_scr[pl.ds(r0, SCAN_ROWS), 0:ns] = xr
        st_scr[pl.ds(r0, SCAN_ROWS), ns:2 * ns] = xi
        return xr, xi

    zero = jnp.zeros((SCAN_ROWS, ns), F32)
    yr, yi = lax.fori_loop(0, n_slabs, slab, (zero, zero), unroll=2)

    cr = carry_scr[0]
    ci = carry_scr[1]
    for step, lag in enumerate((1, 2, 4)):
        mr = tab_ref[2 + 2 * step]
        mi = tab_ref[3 + 2 * step]
        sr = pltpu.roll(yr, lag, axis=0)
        si = pltpu.roll(yi, lag, axis=0)
        yr, yi = yr + mr * sr - mi * si, yi + mr * si + mi * sr
    pr = tab_ref[8]
    pi = tab_ref[9]
    yr, yi = yr + pr * cr - pi * ci, yi + pr * ci + pi * cr
    first = lax.broadcasted_iota(jnp.int32, (SCAN_ROWS, ns), 0) == 0
    init_r = jnp.where(first, cr, pltpu.roll(yr, 1, axis=0))
    init_i = jnp.where(first, ci, pltpu.roll(yi, 1, axis=0))
    last = SCAN_ROWS - 1
    carry_scr[0] = jnp.broadcast_to(yr[last:, :], (SCAN_ROWS, ns))
    carry_scr[1] = jnp.broadcast_to(yi[last:, :], (SCAN_ROWS, ns))

    def fix(s2, _):
        xr, xi = [], []
        for k in range(2):
            s = 2 * s2 + k
            r0 = pl.multiple_of(s * SCAN_ROWS, SCAN_ROWS)
            qr = pw_ref[0, pl.ds(s, 1), :]
            qi = pw_ref[1, pl.ds(s, 1), :]
            xr.append(st_scr[pl.ds(r0, SCAN_ROWS), 0:ns] + qr * init_r - qi * init_i)
            xi.append(st_scr[pl.ds(r0, SCAN_ROWS), ns:2 * ns] + qr * init_i + qi * init_r)
        r2 = pl.multiple_of(s2 * 2 * SCAN_ROWS, 2 * SCAN_ROWS)
        xs_scr[pl.ds(r2, 2 * SCAN_ROWS), 0:ns] = jnp.concatenate(xr, axis=0).astype(BF16)
        xs_scr[pl.ds(r2, 2 * SCAN_ROWS), ns:2 * ns] = jnp.concatenate(xi, axis=0).astype(BF16)
        return 0

    lax.fori_loop(0, n_slabs // 2, fix, 0)

    half_rows = tc // 2
    y_halves = [_dot(xs_scr[k * half_rows:(k + 1) * half_rows, :], wc_ref[...]) for k in range(2)]
    for k, y_half in enumerate(y_halves):
        for c in range(planes):
            y_scr[c, k * half_rows:(k + 1) * half_rows, :] = y_half[:, c * LANES:(c + 1) * LANES]
    for r in range(SCAN_ROWS):
        rows = slice(r * n_slabs, (r + 1) * n_slabs)
        for c in range(planes):
            cols = slice(c * LANES, (c + 1) * LANES)
            y = y_scr[c, pl.ds(r, n_slabs, stride=SCAN_ROWS), :] + dskip_ref[:, cols] * u_block(rows, cols)
            o_ref[rows, cols] = jax.nn.gelu(y, approximate=True).astype(BF16)


def _s5_core(x2d, inv_rms, g, wb, tab, pw, wc, d_skip, *, bsz, seq):
    t, d = x2d.shape
    nblk, cb, ns2 = wb.shape
    n_slabs = pw.shape[2]
    tc = n_slabs * SCAN_ROWS
    assert seq % tc == 0 and n_slabs % 2 == 0 and nblk * cb == d
    nt = seq // tc
    return pl.pallas_call(
        _s5_kernel,
        grid=(bsz, nblk, nt),
        in_specs=[
            pl.BlockSpec((tc, cb), lambda b, gb, tt: (b * nt + tt, gb)),
            pl.BlockSpec((tc, LANES), lambda b, gb, tt: (b * nt + tt, 0)),
            pl.BlockSpec((1, cb), lambda b, gb, tt: (0, gb)),
            pl.BlockSpec((None, cb, ns2), lambda b, gb, tt: (gb, 0, 0)),
            pl.BlockSpec((None, 10, SCAN_ROWS, ns2 // 2), lambda b, gb, tt: (gb, 0, 0, 0)),
            pl.BlockSpec((None, 2, n_slabs, ns2 // 2), lambda b, gb, tt: (gb, 0, 0, 0)),
            pl.BlockSpec((None, ns2, cb), lambda b, gb, tt: (gb, 0, 0)),
            pl.BlockSpec((1, cb), lambda b, gb, tt: (0, gb)),
        ],
        out_specs=pl.BlockSpec((tc, cb), lambda b, gb, tt: (b * nt + tt, gb)),
        out_shape=jax.ShapeDtypeStruct((t, d), BF16),
        scratch_shapes=[
            pltpu.VMEM((cb // LANES, tc, LANES), F32),
            pltpu.VMEM((tc, ns2), F32),
            pltpu.VMEM((tc, ns2), BF16),
            pltpu.VMEM((cb // LANES, tc, LANES), F32),
            pltpu.VMEM((2, SCAN_ROWS, ns2 // 2), F32),
        ],
        compiler_params=_params("arbitrary", "arbitrary", "arbitrary"),
        name="s5_scan",
    )(x2d, inv_rms, g.reshape(1, d), wb, tab, pw, wc, d_skip.reshape(1, d))


def _glu_res_kernel(a_ref, wv_ref, wg_ref, bv_ref, bg_ref, r_ref, o_ref, *, sub):
    a = a_ref[...]
    tn = wv_ref.shape[1]
    pairs = []
    for c in range(tn // sub):
        cols = slice(c * sub, (c + 1) * sub)
        pairs.append((cols, _dot(a, wv_ref[:, cols]), _dot(a, wg_ref[:, cols])))
    for cols, val, gate in pairs:
        o_ref[:, cols] = r_ref[:, cols] + (val + bv_ref[:, cols]) * jax.nn.sigmoid(gate + bg_ref[:, cols])


def _glu_residual(a2d, w, b, res2d, *, tm=1024, tn=1024, sub=256):
    t, k = a2d.shape
    n = w.shape[1] // 2
    assert t % tm == 0 and n % tn == 0 and tn % sub == 0
    nj = n // tn
    return pl.pallas_call(
        functools.partial(_glu_res_kernel, sub=sub),
        grid=(t // tm, nj),
        in_specs=[
            pl.BlockSpec((tm, k), lambda i, j: (i, 0)),
            pl.BlockSpec((k, tn), lambda i, j: (0, j)),
            pl.BlockSpec((k, tn), lambda i, j: (0, nj + j)),
            pl.BlockSpec((1, tn), lambda i, j: (0, j)),
            pl.BlockSpec((1, tn), lambda i, j: (0, nj + j)),
            pl.BlockSpec((tm, tn), lambda i, j: (i, j)),
        ],
        out_specs=pl.BlockSpec((tm, tn), lambda i, j: (i, j)),
        out_shape=jax.ShapeDtypeStruct((t, n), F32),
        compiler_params=_params("arbitrary", "arbitrary"),
        name="glu_residual",
    )(a2d, w, w, b.reshape(1, -1), b.reshape(1, -1), res2d)


def _s5_tables(lam_re, lam_im, log_step, b_re, b_im, c_re, c_im, *, n_slabs):
    groups, p = lam_re.shape
    hc = b_re.shape[2]
    gpb = SSM_GROUPS_PER_BLOCK
    nblk = groups // gpb
    ns = gpb * p
    step = jnp.exp(log_step)[:, None]
    log_a_re = lam_re * step
    log_a_im = lam_im * step
    mag = jnp.exp(log_a_re)
    a_re = mag * jnp.cos(log_a_im)
    a_im = mag * jnp.sin(log_a_im)
    den = lam_re * lam_re + lam_im * lam_im
    f_re = ((a_re - 1.0) * lam_re + a_im * lam_im) / den
    f_im = (a_im * lam_re - (a_re - 1.0) * lam_im) / den
    bb_re = f_re[..., None] * b_re - f_im[..., None] * b_im
    bb_im = f_re[..., None] * b_im + f_im[..., None] * b_re

    same_group = (jnp.arange(gpb * hc)[:, None] // hc) == (jnp.arange(ns)[None, :] // p)

    def group_transpose(m):
        a = m.shape[1]
        eye = jnp.broadcast_to(jnp.eye(m.shape[2], dtype=F32), (nblk, gpb) + (m.shape[2],) * 2)
        return jnp.einsum('bgkq,bgaq->bgka', eye, m.reshape(nblk, gpb, a, m.shape[2]))

    def lane_tile(m):
        copies = jnp.tile(jnp.eye(m.shape[2], dtype=F32), (1, gpb))
        return jnp.einsum('brw,wn->brn', m, copies)

    def block_diag_in(bb):
        rows = group_transpose(bb).reshape(nblk, gpb * hc, p)
        return jnp.where(same_group, lane_tile(rows), 0.0)

    def block_diag_out(c):
        cols = group_transpose(c).reshape(nblk, ns, hc)
        return jnp.where(same_group.T, lane_tile(cols), 0.0)

    wb = jnp.concatenate([block_diag_in(bb_re), block_diag_in(bb_im)], axis=2).astype(BF16)
    wc = jnp.concatenate([block_diag_out(c_re), -block_diag_out(c_im)], axis=1).astype(BF16)

    def powers(exponents):
        k = exponents.astype(F32)[None, :, None]
        k_re = k * log_a_re.reshape(nblk, 1, ns)
        k_im = k * log_a_im.reshape(nblk, 1, ns)
        return jnp.exp(k_re) * jnp.cos(k_im), jnp.exp(k_re) * jnp.sin(k_im)

    fine = 2 * SCAN_ROWS
    assert n_slabs % fine == 0
    lo_re, lo_im = powers(jnp.arange(1, fine + 1))
    hi_re, hi_im = powers(fine * jnp.arange(n_slabs // fine))
    hi_re, hi_im, lo_re, lo_im = (hi_re[:, :, None], hi_im[:, :, None], lo_re[:, None], lo_im[:, None])
    pw = jnp.stack([(hi_re * lo_re - hi_im * lo_im).reshape(nblk, n_slabs, ns),
                    (hi_re * lo_im + hi_im * lo_re).reshape(nblk, n_slabs, ns)], axis=1)
    lr, li = powers(n_slabs * jnp.arange(1, SCAN_ROWS + 1))
    rows = jnp.arange(SCAN_ROWS)[None, :, None]
    ones = jnp.ones((1, SCAN_ROWS, 1), F32)
    tabs = [a_re.reshape(nblk, 1, ns) * ones, a_im.reshape(nblk, 1, ns) * ones]
    for lag in (1, 2, 4):
        for z in (lr[:, lag - 1], li[:, lag - 1]):
            tabs.append(jnp.where(rows >= lag, z[:, None, :], 0.0))
    tabs += [lr, li]
    tab = jnp.stack(tabs, axis=1)
    return wb, tab, pw, wc


def kernel(x, norm_mix_g, norm_ffn_g, pool_w, pool_b, pool_scale, sb_w_qkv, sb_q_gain, sb_k_gain, sb_w_o, ssm_lam_re, ssm_lam_im, ssm_log_step, ssm_b_re, ssm_b_im, ssm_c_re, ssm_c_im, ssm_d, ssm_w_glu, ssm_b_glu, ffn_w_up, ffn_conv_w, ffn_conv_b, ffn_w_down):
    bsz, seq, d = x.shape
    depth = norm_mix_g.shape[0]
    heads = d // SB_HEAD_DIM
    x2d = x.reshape(bsz * seq, d)
    ffn_w_up = ffn_w_up.astype(BF16)
    ffn_w_down = ffn_w_down.astype(BF16)
    ffn_conv_b = ffn_conv_b.reshape(depth, 1, -1)
    for i in range(depth):
        kind = i % 3
        j = i // 3
        if kind == 0:
            x2d = _pool_mixer(x2d, norm_mix_g[i], pool_w[j].astype(BF16), pool_b[j], pool_scale[j],
                              seq=seq)
        elif kind == 1:
            gain_row = jnp.concatenate([
                jnp.tile(sb_q_gain[j] * (math.log2(math.e) / math.sqrt(SB_HEAD_DIM)), heads),
                jnp.tile(sb_k_gain[j], heads),
                jnp.ones((d,), F32)]).reshape(1, 3 * d)
            qkv = _qkv_proj(x2d, norm_mix_g[i], sb_w_qkv[j].astype(BF16), gain_row)
            o = _sb_attention(qkv.reshape(bsz, seq, 3 * d))
            x2d = _proj_residual(o.reshape(bsz * seq, d), sb_w_o[j].astype(BF16), x2d)
        else:
            wb, tab, pw, wc = _s5_tables(ssm_lam_re[j], ssm_lam_im[j], ssm_log_step[j],
                                         ssm_b_re[j], ssm_b_im[j], ssm_c_re[j], ssm_c_im[j],
                                         n_slabs=S5_TIME_CHUNK // SCAN_ROWS)
            y = _s5_core(x2d, _inv_rms(x2d), norm_mix_g[i], wb, tab, pw, wc, ssm_d[j],
                         bsz=bsz, seq=seq)
            x2d = _glu_residual(y, ssm_w_glu[j].astype(BF16), ssm_b_glu[j], x2d)
        x2d = _conv_ffn(x2d, norm_ffn_g[i], ffn_w_up, ffn_conv_w, ffn_conv_b, ffn_w_down,
                        layer=i, seq=seq)
    return x2d.reshape(bsz, seq, d)
```

```python
import functools
import math

import jax
import jax.numpy as jnp
from jax import lax
from jax.experimental import pallas as pl
from jax.experimental.pallas import tpu as pltpu

RMS_EPS = 1e-6
POOL_WINDOWS = (2, 4, 8, 16)
POOL_HALO = 32
SB_HEAD_DIM = 128
SSM_GROUPS_PER_BLOCK = 16
SCAN_ROWS = 8
S5_TIME_CHUNK = 2048
LANES = 128
CONV_WIDTH = 3

V7X_VMEM_LIMIT_BYTES = 60 * 1024 * 1024

F32 = jnp.float32
BF16 = jnp.bfloat16


def _params(*semantics):
    return pltpu.CompilerParams(dimension_semantics=semantics,
                                vmem_limit_bytes=V7X_VMEM_LIMIT_BYTES)


def _rms(x, g):
    ms = jnp.mean(x * x, axis=-1, keepdims=True)
    return x * lax.rsqrt(ms + RMS_EPS) * g


def _dot(a, b):
    return jnp.dot(a, b, preferred_element_type=F32)


def _ffn_kernel(x_ref, g_ref, wv_ref, wg_ref, cwv_ref, cwg_ref, cbv_ref, cbg_ref, wd_ref,
                o_ref, h_scr, act_a, act_b, tail_v, tail_g, *, tiles_per_seq):
    i = pl.program_id(0)
    j = pl.program_id(1)
    n_tiles = pl.num_programs(0) - 1
    nf = pl.num_programs(1)
    tm = x_ref.shape[0]
    fc = wv_ref.shape[1]
    step = i * nf + j

    @pl.when(step == 0)
    def _():
        act_b[...] = jnp.zeros(act_b.shape, BF16)
        o_ref[...] = jnp.zeros(o_ref.shape, F32)

    @pl.when((j == 0) & (i < n_tiles))
    def _():
        h_scr[...] = _rms(x_ref[...], g_ref[...]).astype(BF16)

        @pl.when(i % tiles_per_seq == 0)
        def _():
            tail_v[...] = jnp.zeros(tail_v.shape, F32)
            tail_g[...] = jnp.zeros(tail_g.shape, F32)

    @pl.when((j == 1) & (i < n_tiles))
    def _():
        o_ref[...] = x_ref[...]

    chunk = jnp.where(i < n_tiles, j, nf - 1)
    active = (i < n_tiles) | (j == 0)
    row = lax.broadcasted_iota(jnp.int32, (SCAN_ROWS, fc), 0)

    def causal_conv(up, tail_ref, cw_ref, cb_ref):
        prev = tail_ref[chunk]
        tail_ref[chunk] = up[tm - SCAN_ROWS:, :]
        out = cb_ref[...] + cw_ref[2:3, :] * up
        for lag in (1, 2):
            shifted = pltpu.roll(up, lag, axis=0)
            head = jnp.where(row < lag, pltpu.roll(prev, lag, axis=0), shifted[:SCAN_ROWS])
            shifted = jnp.concatenate([head, shifted[SCAN_ROWS:]], axis=0)
            out = out + cw_ref[2 - lag:3 - lag, :] * shifted
        return out

    def stages(act_new, act_old):
        h = h_scr[...]
        val = causal_conv(_dot(h, wv_ref[...]), tail_v, cwv_ref, cbv_ref)
        up_gate = _dot(h, wg_ref[...])
        o_ref[...] += _dot(act_old[...], wd_ref[...])
        gate = causal_conv(up_gate, tail_g, cwg_ref, cbg_ref)
        half = 0.5 * gate
        act_new[...] = ((half + half * jnp.tanh(half)) * val).astype(BF16)

    pl.when(active & (step % 2 == 0))(lambda: stages(act_a, act_b))
    pl.when(active & (step % 2 == 1))(lambda: stages(act_b, act_a))


def _conv_ffn(x2d, g, w_up, conv_w, conv_b, w_down, *, layer, seq, tm=1024, fc=512):
    t, d = x2d.shape
    f = w_down.shape[1]
    nf = f // fc
    n_tiles = t // tm
    assert t % tm == 0 and seq % tm == 0 and f % fc == 0
    kernel = functools.partial(_ffn_kernel, tiles_per_seq=seq // tm)

    def up_chunk(i, j):
        return jnp.where(i < n_tiles, j, nf - 1)

    def down_chunk(i, j):
        return jnp.where((i < n_tiles) & (j > 0), j - 1, nf - 1)

    def down_tile(i, j):
        return jnp.clip(jnp.where(j == 0, i - 1, i), 0, n_tiles - 1)

    return pl.pallas_call(
        kernel,
        grid=(n_tiles + 1, nf),
        in_specs=[
            pl.BlockSpec((tm, d), lambda i, j: (jnp.minimum(i, n_tiles - 1), 0)),
            pl.BlockSpec((1, d), lambda i, j: (0, 0)),
            pl.BlockSpec((None, d, fc), lambda i, j: (layer, 0, up_chunk(i, j))),
            pl.BlockSpec((None, d, fc), lambda i, j: (layer, 0, nf + up_chunk(i, j))),
            pl.BlockSpec((None, CONV_WIDTH, fc), lambda i, j: (layer, 0, up_chunk(i, j))),
            pl.BlockSpec((None, CONV_WIDTH, fc), lambda i, j: (layer, 0, nf + up_chunk(i, j))),
            pl.BlockSpec((None, 1, fc), lambda i, j: (layer, 0, up_chunk(i, j))),
            pl.BlockSpec((None, 1, fc), lambda i, j: (layer, 0, nf + up_chunk(i, j))),
            pl.BlockSpec((None, fc, d), lambda i, j: (layer, down_chunk(i, j), 0)),
        ],
        out_specs=pl.BlockSpec((tm, d), lambda i, j: (down_tile(i, j), 0)),
        out_shape=jax.ShapeDtypeStruct((t, d), F32),
        scratch_shapes=[
            pltpu.VMEM((tm, d), BF16),
            pltpu.VMEM((tm, fc), BF16),
            pltpu.VMEM((tm, fc), BF16),
            pltpu.VMEM((nf, SCAN_ROWS, fc), F32),
            pltpu.VMEM((nf, SCAN_ROWS, fc), F32),
        ],
        compiler_params=_params("arbitrary", "arbitrary"),
        name="conv_ffn",
    )(x2d, g.reshape(1, d), w_up, w_up, conv_w, conv_w, conv_b, conv_b, w_down)


def _pool_kernel(x_ref, halo_ref, g_ref, w_ref, b_ref, s_ref, o_ref, hbuf, sum_a, sum_b,
                 *, tiles_per_seq):
    i = pl.program_id(0)
    tm, d = x_ref.shape
    dg = d // len(POOL_WINDOWS)
    x = x_ref[...]
    g = g_ref[...]
    h = _rms(x, g)
    tile_in_seq = i % tiles_per_seq
    halo = jnp.where(tile_in_seq == 0, 0.0, _rms(halo_ref[...], g))
    hbuf[0:POOL_HALO, :] = halo
    hbuf[POOL_HALO:, :] = h
    head_rows = 2 * SCAN_ROWS
    assert max(POOL_WINDOWS) <= head_rows and all(w & (w - 1) == 0 for w in POOL_WINDOWS)
    t_in_seq = tile_in_seq * tm + lax.broadcasted_iota(jnp.int32, (head_rows, dg), 0)
    for gi, win in enumerate(POOL_WINDOWS):
        cols = slice(gi * dg, (gi + 1) * dg)
        levels = win.bit_length() - 1
        src, src_cols = hbuf, cols
        for k in range(1, levels + 1):
            lag = 1 << (k - 1)
            start = POOL_HALO - SCAN_ROWS * (levels - k)
            rows = POOL_HALO + tm - start
            acc = src[start:start + rows, src_cols] + src[start - lag:start - lag + rows, src_cols]
            if k < levels:
                src, src_cols = (sum_a, sum_b)[k % 2], slice(0, dg)
                src[start:start + rows, :] = acc
        hg = h[:, cols]
        head_cnt = jnp.minimum(t_in_seq + 1, win).astype(F32)
        pooled = jnp.concatenate([acc[:head_rows] / head_cnt - hg[:head_rows],
                                  acc[head_rows:] * (1.0 / win) - hg[head_rows:]], axis=0)
        y = _dot(pooled.astype(BF16), w_ref[gi])
        o_ref[:, cols] = x[:, cols] + (y + b_ref[:, cols]) * s_ref[:, cols]


def _pool_mixer(x2d, g, w, b, scale, *, seq, tm=512):
    t, d = x2d.shape
    ng, dg, _ = w.shape
    assert t % tm == 0 and seq % tm == 0 and tm % POOL_HALO == 0
    halo_blocks_per_tile = tm // POOL_HALO
    kernel = functools.partial(_pool_kernel, tiles_per_seq=seq // tm)
    return pl.pallas_call(
        kernel,
        grid=(t // tm,),
        in_specs=[
            pl.BlockSpec((tm, d), lambda i: (i, 0)),
            pl.BlockSpec((POOL_HALO, d), lambda i: (jnp.maximum(i * halo_blocks_per_tile - 1, 0), 0)),
            pl.BlockSpec((1, d), lambda i: (0, 0)),
            pl.BlockSpec((ng, dg, dg), lambda i: (0, 0, 0)),
            pl.BlockSpec((1, d), lambda i: (0, 0)),
            pl.BlockSpec((1, d), lambda i: (0, 0)),
        ],
        out_specs=pl.BlockSpec((tm, d), lambda i: (i, 0)),
        out_shape=jax.ShapeDtypeStruct((t, d), F32),
        scratch_shapes=[pltpu.VMEM((POOL_HALO + tm, d), F32),
                        pltpu.VMEM((POOL_HALO + tm, dg), F32),
                        pltpu.VMEM((POOL_HALO + tm, dg), F32)],
        compiler_params=_params("arbitrary"),
        name="pool_mixer",
    )(x2d, x2d, g.reshape(1, d), w, b.reshape(1, d), scale.reshape(1, d))


def _qkv_kernel(x_ref, g_ref, w_ref, gain_ref, o_ref, h_scr, *, norm_tiles, sub):
    j = pl.program_id(1)
    tn = w_ref.shape[1]

    @pl.when(j == 0)
    def _():
        h_scr[...] = _rms(x_ref[...], g_ref[...]).astype(BF16)

    h = h_scr[...]
    normed = j < norm_tiles
    accs = [_dot(h, w_ref[:, c * sub:(c + 1) * sub]) for c in range(tn // sub)]
    for c, acc in enumerate(accs):
        for hh in range(sub // SB_HEAD_DIM):
            head = acc[:, hh * SB_HEAD_DIM:(hh + 1) * SB_HEAD_DIM]
            cols = slice(c * sub + hh * SB_HEAD_DIM, c * sub + (hh + 1) * SB_HEAD_DIM)
            o_ref[:, cols] = jnp.where(normed, _rms(head, gain_ref[:, cols]), head).astype(BF16)


def _qkv_proj(x2d, g, w_qkv, gain_row, *, tm=1024, tn=2048, sub=512):
    t, d = x2d.shape
    n = w_qkv.shape[1]
    assert t % tm == 0 and n % tn == 0 and (2 * d) % tn == 0 and sub % SB_HEAD_DIM == 0
    assert tn % sub == 0
    kernel = functools.partial(_qkv_kernel, norm_tiles=2 * d // tn, sub=sub)
    return pl.pallas_call(
        kernel,
        grid=(t // tm, n // tn),
        in_specs=[
            pl.BlockSpec((tm, d), lambda i, j: (i, 0)),
            pl.BlockSpec((1, d), lambda i, j: (0, 0)),
            pl.BlockSpec((d, tn), lambda i, j: (0, j)),
            pl.BlockSpec((1, tn), lambda i, j: (0, j)),
        ],
        out_specs=pl.BlockSpec((tm, tn), lambda i, j: (i, j)),
        out_shape=jax.ShapeDtypeStruct((t, n), BF16),
        scratch_shapes=[pltpu.VMEM((tm, d), BF16)],
        compiler_params=_params("arbitrary", "arbitrary"),
        name="sb_qkv",
    )(x2d, g.reshape(1, d), w_qkv, gain_row)


def _sb_attn_kernel(q_ref, k_ref, v_ref, o_ref, *, chain_rows):
    qi = pl.program_id(2)
    blk = q_ref.shape[0]
    dh = SB_HEAD_DIM
    n_heads = q_ref.shape[1] // dh
    row = lax.broadcasted_iota(jnp.int32, (blk, blk), 0)
    col = lax.broadcasted_iota(jnp.int32, (blk, blk), 1)
    later = jnp.where(row > col, 1.0, 0.0).astype(BF16)
    later2 = jnp.concatenate([later, later], axis=0)
    row_g = lax.broadcasted_iota(jnp.int32, (chain_rows, blk), 0)
    col_g = lax.broadcasted_iota(jnp.int32, (chain_rows, blk), 1)

    chains = [(hh, r0) for hh in range(n_heads) for r0 in range(0, blk, chain_rows)]

    def key_block(kb, state, diagonal):
        start = pl.multiple_of(kb * blk, blk)
        cols = [slice(hh * dh, (hh + 1) * dh) for hh, _ in chains]
        rows = [slice(r0, r0 + chain_rows) for _, r0 in chains]
        zs = [lax.dot_general(q_ref[r, c], k_ref[pl.ds(start, blk), c], (((1,), (1,)), ((), ())),
                              preferred_element_type=F32) for r, c in zip(rows, cols)]
        log_betas, log_1m_betas, splits = [], [], []
        for (_, r0), z in zip(chains, zs):
            log_beta = jnp.minimum(z, 0.0) - jnp.log2(1.0 + jnp.exp2(-jnp.abs(z)))
            log_1m_beta = log_beta - z
            if diagonal:
                log_1m_beta = jnp.where(col_g < row_g + r0, log_1m_beta, 0.0)
            hi = log_1m_beta.astype(BF16)
            lo = (log_1m_beta - hi.astype(F32)).astype(BF16)
            log_betas.append(log_beta)
            log_1m_betas.append(log_1m_beta)
            splits.append(jnp.concatenate([hi, lo], axis=1))
        remains = [_dot(s, later2) for s in splits]
        ws = []
        for (_, r0), log_beta, remain, (carry, _) in zip(chains, log_betas, remains, state):
            w = jnp.exp2(log_beta + remain + carry)
            if diagonal:
                w = jnp.where(col_g < row_g + r0, w, 0.0)
            ws.append(w.astype(BF16))
        out = []
        for c, w, log_1m_beta, (carry, acc) in zip(cols, ws, log_1m_betas, state):
            out.append((carry + jnp.sum(log_1m_beta, axis=-1, keepdims=True),
                        acc + _dot(w, v_ref[pl.ds(start, blk), c])))
        return tuple(out)

    zero = (jnp.zeros((chain_rows, 1), F32), jnp.zeros((chain_rows, dh), F32))
    state = key_block(qi, (zero,) * len(chains), True)
    state = lax.fori_loop(0, qi, lambda n, st: key_block(qi - 1 - n, st, False), state)
    for (hh, r0), (_, acc) in zip(chains, state):
        o_ref[r0:r0 + chain_rows, hh * dh:(hh + 1) * dh] = acc.astype(BF16)


def _sb_attention(qkv3d, *, blk=256, heads_per_step=16, chain_rows=256):
    bsz, seq, n3 = qkv3d.shape
    d = n3 // 3
    wide = heads_per_step * SB_HEAD_DIM
    groups = d // wide
    assert seq % blk == 0 and d % wide == 0 and blk % chain_rows == 0
    return pl.pallas_call(
        functools.partial(_sb_attn_kernel, chain_rows=chain_rows),
        grid=(bsz, groups, seq // blk),
        in_specs=[
            pl.BlockSpec((None, blk, wide), lambda b, h, qi: (b, qi, h)),
            pl.BlockSpec((None, seq, wide), lambda b, h, qi: (b, 0, groups + h),
                         pipeline_mode=pl.Buffered(1)),
            pl.BlockSpec((None, seq, wide), lambda b, h, qi: (b, 0, 2 * groups + h),
                         pipeline_mode=pl.Buffered(1)),
        ],
        out_specs=pl.BlockSpec((None, blk, wide), lambda b, h, qi: (b, qi, h)),
        out_shape=jax.ShapeDtypeStruct((bsz, seq, d), BF16),
        compiler_params=_params("arbitrary", "arbitrary", "arbitrary"),
        name="sb_attention",
    )(qkv3d, qkv3d, qkv3d)


def _proj_res_kernel(a_ref, w_ref, r_ref, o_ref, *, sub):
    a = a_ref[...]
    tn = w_ref.shape[1]
    accs = [_dot(a, w_ref[:, c * sub:(c + 1) * sub]) for c in range(tn // sub)]
    for c, acc in enumerate(accs):
        cols = slice(c * sub, (c + 1) * sub)
        o_ref[:, cols] = r_ref[:, cols] + acc


def _proj_residual(a2d, w, res2d, *, tm=1024, tn=2048, sub=512):
    t, k = a2d.shape
    n = w.shape[1]
    assert t % tm == 0 and n % tn == 0 and tn % sub == 0
    return pl.pallas_call(
        functools.partial(_proj_res_kernel, sub=sub),
        grid=(t // tm, n // tn),
        in_specs=[
            pl.BlockSpec((tm, k), lambda i, j: (i, 0)),
            pl.BlockSpec((k, tn), lambda i, j: (0, j)),
            pl.BlockSpec((tm, tn), lambda i, j: (i, j)),
        ],
        out_specs=pl.BlockSpec((tm, tn), lambda i, j: (i, j)),
        out_shape=jax.ShapeDtypeStruct((t, n), F32),
        compiler_params=_params("arbitrary", "arbitrary"),
        name="proj_residual",
    )(a2d, w, res2d)


def _inv_rms_kernel(x_ref, o_ref):
    x = x_ref[...]
    inv = lax.rsqrt(jnp.mean(x * x, axis=-1, keepdims=True) + RMS_EPS)
    o_ref[...] = jnp.broadcast_to(inv, o_ref.shape)


def _inv_rms(x2d, *, tm=512):
    t, d = x2d.shape
    assert t % tm == 0
    return pl.pallas_call(
        _inv_rms_kernel,
        grid=(t // tm,),
        in_specs=[pl.BlockSpec((tm, d), lambda i: (i, 0))],
        out_specs=pl.BlockSpec((tm, LANES), lambda i: (i, 0)),
        out_shape=jax.ShapeDtypeStruct((t, LANES), F32),
        compiler_params=_params("arbitrary"),
        name="inv_rms",
    )(x2d)


def _s5_kernel(x_ref, inv_ref, g_ref, wb_ref, tab_ref, pw_ref, wc_ref, dskip_ref, o_ref,
               up_scr, st_scr, xs_scr, y_scr, carry_scr):
    tc = x_ref.shape[0]
    ns = wb_ref.shape[1] // 2
    n_slabs = tc // SCAN_ROWS

    def u_block(rows, cols):
        return x_ref[rows, cols] * inv_ref[rows, :] * g_ref[:, cols]

    planes = up_scr.shape[0]
    for r in range(SCAN_ROWS):
        for c in range(planes):
            up_scr[c, pl.ds(r, n_slabs, stride=SCAN_ROWS), :] = u_block(
                slice(r * n_slabs, (r + 1) * n_slabs), slice(c * LANES, (c + 1) * LANES))
    u_perm = jnp.concatenate([up_scr[c] for c in range(planes)], axis=1)
    st_scr[...] = _dot(u_perm.astype(BF16), wb_ref[...])

    @pl.when(pl.program_id(2) == 0)
    def _():
        carry_scr[...] = jnp.zeros(carry_scr.shape, F32)

    ar = tab_ref[0]
    ai = tab_ref[1]

    def slab(s, x):
        xr, xi = x
        r0 = pl.multiple_of(s * SCAN_ROWS, SCAN_ROWS)
        xr, xi = (ar * xr - ai * xi + st_scr[pl.ds(r0, SCAN_ROWS), 0:ns],
                  ar * xi + ai * xr + st_scr[pl.ds(r0, SCAN_ROWS), ns:2 * ns])
        st_scr[pl.ds(r0, SCAN_ROWS), 0:ns] = xr
        st_scr[pl.ds(r0, SCAN_ROWS), ns:2 * ns] = xi
        return xr, xi

    zero = jnp.zeros((SCAN_ROWS, ns), F32)
    yr, yi = lax.fori_loop(0, n_slabs, slab, (zero, zero), unroll=2)

    cr = carry_scr[0]
    ci = carry_scr[1]
    for step, lag in enumerate((1, 2, 4)):
        mr = tab_ref[2 + 2 * step]
        mi = tab_ref[3 + 2 * step]
        sr = pltpu.roll(yr, lag, axis=0)
        si = pltpu.roll(yi, lag, axis=0)
        yr, yi = yr + mr * sr - mi * si, yi + mr * si + mi * sr
    pr = tab_ref[8]
    pi = tab_ref[9]
    yr, yi = yr + pr * cr - pi * ci, yi + pr * ci + pi * cr
    first = lax.broadcasted_iota(jnp.int32, (SCAN_ROWS, ns), 0) == 0
    init_r = jnp.where(first, cr, pltpu.roll(yr, 1, axis=0))
    init_i = jnp.where(first, ci, pltpu.roll(yi, 1, axis=0))
    last = SCAN_ROWS - 1
    carry_scr[0] = jnp.broadcast_to(yr[last:, :], (SCAN_ROWS, ns))
    carry_scr[1] = jnp.broadcast_to(yi[last:, :], (SCAN_ROWS, ns))

    def fix(s2, _):
        xr, xi = [], []
        for k in range(2):
            s = 2 * s2 + k
            r0 = pl.multiple_of(s * SCAN_ROWS, SCAN_ROWS)
            qr = pw_ref[0, pl.ds(s, 1), :]
            qi = pw_ref[1, pl.ds(s, 1), :]
            xr.append(st_scr[pl.ds(r0, SCAN_ROWS), 0:ns] + qr * init_r - qi * init_i)
            xi.append(st_scr[pl.ds(r0, SCAN_ROWS), ns:2 * ns] + qr * init_i + qi * init_r)
        r2 = pl.multiple_of(s2 * 2 * SCAN_ROWS, 2 * SCAN_ROWS)
        xs_scr[pl.ds(r2, 2 * SCAN_ROWS), 0:ns] = jnp.concatenate(xr, axis=0).astype(BF16)
        xs_scr[pl.ds(r2, 2 * SCAN_ROWS), ns:2 * ns] = jnp.concatenate(xi, axis=0).astype(BF16)
        return 0

    lax.fori_loop(0, n_slabs // 2, fix, 0)

    half_rows = tc // 2
    y_halves = [_dot(xs_scr[k * half_rows:(k + 1) * half_rows, :], wc_ref[...]) for k in range(2)]
    for k, y_half in enumerate(y_halves):
        for c in range(planes):
            y_scr[c, k * half_rows:(k + 1) * half_rows, :] = y_half[:, c * LANES:(c + 1) * LANES]
    for r in range(SCAN_ROWS):
        rows = slice(r * n_slabs, (r + 1) * n_slabs)
        for c in range(planes):
            cols = slice(c * LANES, (c + 1) * LANES)
            y = y_scr[c, pl.ds(r, n_slabs, stride=SCAN_ROWS), :] + dskip_ref[:, cols] * u_block(rows, cols)
            o_ref[rows, cols] = jax.nn.gelu(y, approximate=True).astype(BF16)


def _s5_core(x2d, inv_rms, g, wb, tab, pw, wc, d_skip, *, bsz, seq):
    t, d = x2d.shape
    nblk, cb, ns2 = wb.shape
    n_slabs = pw.shape[2]
    tc = n_slabs * SCAN_ROWS
    assert seq % tc == 0 and n_slabs % 2 == 0 and nblk * cb == d
    nt = seq // tc
    return pl.pallas_call(
        _s5_kernel,
        grid=(bsz, nblk, nt),
        in_specs=[
            pl.BlockSpec((tc, cb), lambda b, gb, tt: (b * nt + tt, gb)),
            pl.BlockSpec((tc, LANES), lambda b, gb, tt: (b * nt + tt, 0)),
            pl.BlockSpec((1, cb), lambda b, gb, tt: (0, gb)),
            pl.BlockSpec((None, cb, ns2), lambda b, gb, tt: (gb, 0, 0)),
            pl.BlockSpec((None, 10, SCAN_ROWS, ns2 // 2), lambda b, gb, tt: (gb, 0, 0, 0)),
            pl.BlockSpec((None, 2, n_slabs, ns2 // 2), lambda b, gb, tt: (gb, 0, 0, 0)),
            pl.BlockSpec((None, ns2, cb), lambda b, gb, tt: (gb, 0, 0)),
            pl.BlockSpec((1, cb), lambda b, gb, tt: (0, gb)),
        ],
        out_specs=pl.BlockSpec((tc, cb), lambda b, gb, tt: (b * nt + tt, gb)),
        out_shape=jax.ShapeDtypeStruct((t, d), BF16),
        scratch_shapes=[
            pltpu.VMEM((cb // LANES, tc, LANES), F32),
            pltpu.VMEM((tc, ns2), F32),
            pltpu.VMEM((tc, ns2), BF16),
            pltpu.VMEM((cb // LANES, tc, LANES), F32),
            pltpu.VMEM((2, SCAN_ROWS, ns2 // 2), F32),
        ],
        compiler_params=_params("arbitrary", "arbitrary", "arbitrary"),
        name="s5_scan",
    )(x2d, inv_rms, g.reshape(1, d), wb, tab, pw, wc, d_skip.reshape(1, d))


def _glu_res_kernel(a_ref, wv_ref, wg_ref, bv_ref, bg_ref, r_ref, o_ref, *, sub):
    a = a_ref[...]
    tn = wv_ref.shape[1]
    pairs = []
    for c in range(tn // sub):
        cols = slice(c * sub, (c + 1) * sub)
        pairs.append((cols, _dot(a, wv_ref[:, cols]), _dot(a, wg_ref[:, cols])))
    for cols, val, gate in pairs:
        o_ref[:, cols] = r_ref[:, cols] + (val + bv_ref[:, cols]) * jax.nn.sigmoid(gate + bg_ref[:, cols])


def _glu_residual(a2d, w, b, res2d, *, tm=1024, tn=1024, sub=256):
    t, k = a2d.shape
    n = w.shape[1] // 2
    assert t % tm == 0 and n % tn == 0 and tn % sub == 0
    nj = n // tn
    return pl.pallas_call(
        functools.partial(_glu_res_kernel, sub=sub),
        grid=(t // tm, nj),
        in_specs=[
            pl.BlockSpec((tm, k), lambda i, j: (i, 0)),
            pl.BlockSpec((k, tn), lambda i, j: (0, j)),
            pl.BlockSpec((k, tn), lambda i, j: (0, nj + j)),
            pl.BlockSpec((1, tn), lambda i, j: (0, j)),
            pl.BlockSpec((1, tn), lambda i, j: (0, nj + j)),
            pl.BlockSpec((tm, tn), lambda i, j: (i, j)),
        ],
        out_specs=pl.BlockSpec((tm, tn), lambda i, j: (i, j)),
        out_shape=jax.ShapeDtypeStruct((t, n), F32),
        compiler_params=_params("arbitrary", "arbitrary"),
        name="glu_residual",
    )(a2d, w, w, b.reshape(1, -1), b.reshape(1, -1), res2d)


def _s5_tables(lam_re, lam_im, log_step, b_re, b_im, c_re, c_im, *, n_slabs):
    groups, p = lam_re.shape
    hc = b_re.shape[2]
    gpb = SSM_GROUPS_PER_BLOCK
    nblk = groups // gpb
    ns = gpb * p
    step = jnp.exp(log_step)[:, None]
    log_a_re = lam_re * step
    log_a_im = lam_im * step
    mag = jnp.exp(log_a_re)
    a_re = mag * jnp.cos(log_a_im)
    a_im = mag * jnp.sin(log_a_im)
    den = lam_re * lam_re + lam_im * lam_im
    f_re = ((a_re - 1.0) * lam_re + a_im * lam_im) / den
    f_im = (a_im * lam_re - (a_re - 1.0) * lam_im) / den
    bb_re = f_re[..., None] * b_re - f_im[..., None] * b_im
    bb_im = f_re[..., None] * b_im + f_im[..., None] * b_re

    same_group = (jnp.arange(gpb * hc)[:, None] // hc) == (jnp.arange(ns)[None, :] // p)

    def group_transpose(m):
        a = m.shape[1]
        eye = jnp.broadcast_to(jnp.eye(m.shape[2], dtype=F32), (nblk, gpb) + (m.shape[2],) * 2)
        return jnp.einsum('bgkq,bgaq->bgka', eye, m.reshape(nblk, gpb, a, m.shape[2]))

    def lane_tile(m):
        copies = jnp.tile(jnp.eye(m.shape[2], dtype=F32), (1, gpb))
        return jnp.einsum('brw,wn->brn', m, copies)

    def block_diag_in(bb):
        rows = group_transpose(bb).reshape(nblk, gpb * hc, p)
        return jnp.where(same_group, lane_tile(rows), 0.0)

    def block_diag_out(c):
        cols = group_transpose(c).reshape(nblk, ns, hc)
        return jnp.where(same_group.T, lane_tile(cols), 0.0)

    wb = jnp.concatenate([block_diag_in(bb_re), block_diag_in(bb_im)], axis=2).astype(BF16)
    wc = jnp.concatenate([block_diag_out(c_re), -block_diag_out(c_im)], axis=1).astype(BF16)

    def powers(exponents):
        k = exponents.astype(F32)[None, :, None]
        k_re = k * log_a_re.reshape(nblk, 1, ns)
        k_im = k * log_a_im.reshape(nblk, 1, ns)
        return jnp.exp(k_re) * jnp.cos(k_im), jnp.exp(k_re) * jnp.sin(k_im)

    fine = 2 * SCAN_ROWS
    assert n_slabs % fine == 0
    lo_re, lo_im = powers(jnp.arange(1, fine + 1))
    hi_re, hi_im = powers(fine * jnp.arange(n_slabs // fine))
    hi_re, hi_im, lo_re, lo_im = (hi_re[:, :, None], hi_im[:, :, None], lo_re[:, None], lo_im[:, None])
    pw = jnp.stack([(hi_re * lo_re - hi_im * lo_im).reshape(nblk, n_slabs, ns),
                    (hi_re * lo_im + hi_im * lo_re).reshape(nblk, n_slabs, ns)], axis=1)
    lr, li = powers(n_slabs * jnp.arange(1, SCAN_ROWS + 1))
    rows = jnp.arange(SCAN_ROWS)[None, :, None]
    ones = jnp.ones((1, SCAN_ROWS, 1), F32)
    tabs = [a_re.reshape(nblk, 1, ns) * ones, a_im.reshape(nblk, 1, ns) * ones]
    for lag in (1, 2, 4):
        for z in (lr[:, lag - 1], li[:, lag - 1]):
            tabs.append(jnp.where(rows >= lag, z[:, None, :], 0.0))
    tabs += [lr, li]
    tab = jnp.stack(tabs, axis=1)
    return wb, tab, pw, wc


def kernel(x, norm_mix_g, norm_ffn_g, pool_w, pool_b, pool_scale, sb_w_qkv, sb_q_gain, sb_k_gain, sb_w_o, ssm_lam_re, ssm_lam_im, ssm_log_step, ssm_b_re, ssm_b_im, ssm_c_re, ssm_c_im, ssm_d, ssm_w_glu, ssm_b_glu, ffn_w_up, ffn_conv_w, ffn_conv_b, ffn_w_down):
    bsz, seq, d = x.shape
    depth = norm_mix_g.shape[0]
    heads = d // SB_HEAD_DIM
    x2d = x.reshape(bsz * seq, d)
    ffn_w_up = ffn_w_up.astype(BF16)
    ffn_w_down = ffn_w_down.astype(BF16)
    ffn_conv_b = ffn_conv_b.reshape(depth, 1, -1)
    for i in range(depth):
        kind = i % 3
        j = i // 3
        if kind == 0:
            x2d = _pool_mixer(x2d, norm_mix_g[i], pool_w[j].astype(BF16), pool_b[j], pool_scale[j],
                              seq=seq)
        elif kind == 1:
            gain_row = jnp.concatenate([
                jnp.tile(sb_q_gain[j] * (math.log2(math.e) / math.sqrt(SB_HEAD_DIM)), heads),
                jnp.tile(sb_k_gain[j], heads),
                jnp.ones((d,), F32)]).reshape(1, 3 * d)
            qkv = _qkv_proj(x2d, norm_mix_g[i], sb_w_qkv[j].astype(BF16), gain_row)
            o = _sb_attention(qkv.reshape(bsz, seq, 3 * d))
            x2d = _proj_residual(o.reshape(bsz * seq, d), sb_w_o[j].astype(BF16), x2d)
        else:
            wb, tab, pw, wc = _s5_tables(ssm_lam_re[j], ssm_lam_im[j], ssm_log_step[j],
                                         ssm_b_re[j], ssm_b_im[j], ssm_c_re[j], ssm_c_im[j],
                                         n_slabs=S5_TIME_CHUNK // SCAN_ROWS)
            y = _s5_core(x2d, _inv_rms(x2d), norm_mix_g[i], wb, tab, pw, wc, ssm_d[j],
                         bsz=bsz, seq=seq)
            x2d = _glu_residual(y, ssm_w_glu[j].astype(BF16), ssm_b_glu[j], x2d)
        x2d = _conv_ffn(x2d, norm_ffn_g[i], ffn_w_up, ffn_conv_w, ffn_conv_b, ffn_w_down,
                        layer=i, seq=seq)
    return x2d.reshape(bsz, seq, d)
```
